```python
import math
import jax
import jax.numpy as jnp
from jax import lax
import numpy as np

D_MODEL = 1024
BATCH = 2
SEQ = 8192
DEPTH = 4
DEC_BATCH = 128
DEC_SEQ = 1
PAST_LEN = 2048
PAGE_SIZE = 128

N_A_LAYERS = (DEPTH + 1) // 2
N_B_LAYERS = DEPTH // 2

GDN_HK = 8
GDN_HV = 16
GDN_DK = 128
GDN_DV = 128
GDN_KEY_DIM = GDN_HK * GDN_DK
GDN_VAL_DIM = GDN_HV * GDN_DV
GDN_QKV_DIM = 2 * GDN_KEY_DIM + GDN_VAL_DIM
GDN_IN_DIM = GDN_QKV_DIM + GDN_VAL_DIM + 2 * GDN_HV
CONV_W = 4
GDN_CHUNK = 64

ATT_H = 16
ATT_DH = 64
ATT_DIM = ATT_H * ATT_DH
DIL_GROUPS = ((128, 1), (512, 4), (2048, 16))
N_GROUPS = len(DIL_GROUPS)
ATT_IN_DIM = N_GROUPS * 3 * ATT_DIM
ATT_BLOCK = 128
ROPE_THETA = 10000.0

D_FF = 3584
N_EXPERTS = 8
TOP_K = 2

EPS = 1e-6
NEG_INF = -1e30

kernel_name = 'hybrid_gdn_dilated_swa_adaln_step'


def rms_norm(x, w):
    xf = x.astype(jnp.float32)
    y = xf * lax.rsqrt(jnp.mean(xf * xf, axis=-1, keepdims=True) + EPS)
    return (y * w.astype(jnp.float32)).astype(x.dtype)


def l2_normalize(x):
    return x * lax.rsqrt(jnp.sum(x * x, axis=-1, keepdims=True) + EPS)


def rope(x, pos):
    half = x.shape[-1] // 2
    inv_freq = ROPE_THETA ** (-jnp.arange(half, dtype=jnp.float32) / half)
    ang = pos.astype(jnp.float32)[:, None] * inv_freq[None, :]
    cos = jnp.cos(ang)[None, :, None, :]
    sin = jnp.sin(ang)[None, :, None, :]
    xf = x.astype(jnp.float32)
    x1, x2 = xf[..., :half], xf[..., half:]
    return jnp.concatenate([x1 * cos - x2 * sin, x2 * cos + x1 * sin], axis=-1).astype(x.dtype)


def causal_short_conv(x, buf, w):
    T = x.shape[1]
    xp = jnp.concatenate([buf.astype(x.dtype), x], axis=1)
    y = xp[:, 0:T] * w[0]
    for i in range(1, CONV_W):
        y = y + xp[:, i:i + T] * w[i]
    return jax.nn.silu(y), xp[:, -(CONV_W - 1):]


def gated_delta_rule(q, k, v, g, beta, s0):
    Bn, T, H, DK = q.shape
    DV = v.shape[-1]
    C = min(GDN_CHUNK, T)
    pad = (-T) % C
    N = (T + pad) // C

    def chunks(t):
        t = jnp.swapaxes(t, 1, 2)
        t = jnp.pad(t, [(0, 0), (0, 0), (0, pad)] + [(0, 0)] * (t.ndim - 3))
        t = t.reshape(t.shape[:2] + (N, C) + t.shape[3:])
        return jnp.moveaxis(t, 2, 0)

    qc, kc, vc, gc, bc = chunks(q), chunks(k), chunks(v), chunks(g), chunks(beta)
    gcum = jnp.cumsum(gc, axis=-1)
    ii = jnp.arange(C)[:, None]
    jj = jnp.arange(C)[None, :]
    causal = ii >= jj
    strict = (ii > jj).astype(jnp.float32)
    diff = gcum[..., :, None] - gcum[..., None, :]
    decay = jnp.where(causal, jnp.exp(jnp.where(causal, diff, 0.0)), 0.0)
    kb = kc * bc[..., None]
    vb = vc * bc[..., None]
    lmat = jnp.einsum('nbhid,nbhjd->nbhij', kb, kc) * decay * strict
    amat = jnp.eye(C, dtype=jnp.float32) + lmat
    rhs = jnp.concatenate([vb, kb * jnp.exp(gcum)[..., None]], axis=-1)
    sol = lax.linalg.triangular_solve(amat, rhs, left_side=True, lower=True, unit_diagonal=True)
    u, w = sol[..., :DV], sol[..., DV:]
    qk = jnp.einsum('nbhid,nbhjd->nbhij', qc, kc) * decay
    q_dec = qc * jnp.exp(gcum)[..., None]
    k_dec = kc * jnp.exp(gcum[..., -1:] - gcum)[..., None]
    g_last = jnp.exp(gcum[..., -1])

    def step(S, xs):
        u_n, w_n, qk_n, qd_n, kd_n, gl_n = xs
        v_new = u_n - jnp.einsum('bhck,bhkv->bhcv', w_n, S)
        o = jnp.einsum('bhck,bhkv->bhcv', qd_n, S) + jnp.einsum('bhij,bhjv->bhiv', qk_n, v_new)
        S = S * gl_n[..., None, None] + jnp.einsum('bhck,bhcv->bhkv', kd_n, v_new)
        return S, o

    S, o = lax.scan(step, s0, (u, w, qk, q_dec, k_dec, g_last))
    o = jnp.moveaxis(o, 0, 2).reshape(Bn, H, N * C, DV)[:, :, :T]
    return jnp.swapaxes(o, 1, 2), S


def gdn_mixer(h, w_in, conv_w, a_log, dt_bias, norm_o, w_out, conv_buf, s0):
    Bn, T, _ = h.shape
    proj = h @ w_in
    qkv = proj[..., :GDN_QKV_DIM]
    z = proj[..., GDN_QKV_DIM:GDN_QKV_DIM + GDN_VAL_DIM]
    a = proj[..., GDN_QKV_DIM + GDN_VAL_DIM:GDN_QKV_DIM + GDN_VAL_DIM + GDN_HV]
    b = proj[..., GDN_QKV_DIM + GDN_VAL_DIM + GDN_HV:]
    qkv_c, new_buf = causal_short_conv(qkv, conv_buf, conv_w)
    qkv_c = qkv_c.astype(jnp.float32)
    q = qkv_c[..., :GDN_KEY_DIM].reshape(Bn, T, GDN_HK, GDN_DK)
    k = qkv_c[..., GDN_KEY_DIM:2 * GDN_KEY_DIM].reshape(Bn, T, GDN_HK, GDN_DK)
    v = qkv_c[..., 2 * GDN_KEY_DIM:].reshape(Bn, T, GDN_HV, GDN_DV)
    q = jnp.repeat(l2_normalize(q) * (GDN_DK ** -0.5), GDN_HV // GDN_HK, axis=2)
    k = jnp.repeat(l2_normalize(k), GDN_HV // GDN_HK, axis=2)
    beta = jax.nn.sigmoid(b.astype(jnp.float32))
    g = -jnp.exp(a_log.astype(jnp.float32)) * jax.nn.softplus(
        a.astype(jnp.float32) + dt_bias.astype(jnp.float32))
    o, S = gated_delta_rule(q, k, v, g, beta, s0.astype(jnp.float32))
    o = rms_norm(o, norm_o) * jax.nn.silu(z.astype(jnp.float32).reshape(Bn, T, GDN_HV, GDN_DV))
    y = o.astype(h.dtype).reshape(Bn, T, GDN_VAL_DIM) @ w_out
    return y, new_buf, S.astype(s0.dtype)


def dilated_attn_prompt(q, k, v, dil, n_win):
    Bn, T, H, Dh = q.shape
    Td = T // dil
    Tp = -(-Td // ATT_BLOCK) * ATT_BLOCK
    nb = Tp // ATT_BLOCK
    Z = Bn * dil

    def by_stride(t):
        t = t.reshape(Bn, Td, dil, H, Dh).transpose(0, 2, 1, 3, 4).reshape(Z, Td, H, Dh)
        return jnp.pad(t, ((0, 0), (0, Tp - Td), (0, 0), (0, 0)))

    def band(t):
        tp = jnp.pad(t, ((0, 0), (ATT_BLOCK, 0), (0, 0), (0, 0))).reshape(Z, nb + 1, ATT_BLOCK, H, Dh)
        return jnp.concatenate([tp[:, :-1], tp[:, 1:]], axis=2)

    qb = by_stride(q).reshape(Z, nb, ATT_BLOCK, H, Dh)
    kb = band(by_stride(k))
    vb = band(by_stride(v))
    qi = jnp.arange(ATT_BLOCK)[:, None]
    kj = jnp.arange(2 * ATT_BLOCK)[None, :]
    dist = qi + ATT_BLOCK - kj
    key_pos = jnp.arange(nb)[:, None, None] * ATT_BLOCK + kj - ATT_BLOCK
    mask = (dist >= 0) & (dist <= n_win) & (key_pos >= 0)
    s = jnp.einsum('znqhd,znkhd->znhqk', qb, kb, preferred_element_type=jnp.float32) * (Dh ** -0.5)
    s = jnp.where(mask[None, :, None], s, NEG_INF)
    lse = jax.nn.logsumexp(s, axis=-1)
    p = jnp.exp(s - lse[..., None])
    o = jnp.einsum('znhqk,znkhd->znqhd', p.astype(v.dtype), vb)
    o = o.reshape(Bn, dil, Tp, H, Dh)[:, :, :Td].transpose(0, 2, 1, 3, 4).reshape(Bn, T, H, Dh)
    lse = lse.transpose(0, 1, 3, 2).reshape(Bn, dil, Tp, H)[:, :, :Td]
    lse = lse.transpose(0, 2, 1, 3).reshape(Bn, T, H)
    return o, lse


def dilated_attn_step(q, kv_new, kv_buf, dil, n_win):
    L = kv_buf.shape[1]
    T = q.shape[1]
    Dh = q.shape[-1]
    kv = jnp.concatenate([kv_buf.astype(kv_new.dtype), kv_new], axis=1)
    idx = (L + jnp.arange(T))[:, None] - dil * jnp.arange(n_win + 1)[None, :]
    valid = idx >= 0
    kvg = jnp.take(kv, jnp.maximum(idx, 0), axis=1)
    s = jnp.einsum('bthd,btjhd->bhtj', q, kvg[:, :, :, 0],
                   preferred_element_type=jnp.float32) * (Dh ** -0.5)
    s = jnp.where(valid[None, None], s, NEG_INF)
    lse = jax.nn.logsumexp(s, axis=-1)
    p = jnp.exp(s - lse[..., None])
    o = jnp.einsum('bhtj,btjhd->bthd', p.astype(q.dtype), kvg[:, :, :, 1])
    return o, jnp.transpose(lse, (0, 2, 1))


def dilated_mixer(h, pos, w_in, w_out, kv_bufs):
    Bn, T, _ = h.shape
    proj = (h @ w_in).reshape(Bn, T, N_GROUPS, 3, ATT_H, ATT_DH)
    outs, lses, rows = [], [], []
    for gi, (window, dil) in enumerate(DIL_GROUPS):
        n_win = window // dil
        q = rope(proj[:, :, gi, 0], pos)
        k = rope(proj[:, :, gi, 1], pos)
        v = proj[:, :, gi, 2]
        kv = jnp.stack([k, v], axis=2)
        if kv_bufs is None:
            o, lse = dilated_attn_prompt(q, k, v, dil, n_win)
            rows.append(kv[:, -min(window, T):])
        else:
            o, lse = dilated_attn_step(q, kv, kv_bufs[gi], dil, n_win)
            rows.append(kv)
        outs.append(o.astype(jnp.float32))
        lses.append(lse)
    wgt = jax.nn.softmax(jnp.stack(lses, axis=0), axis=0)
    o = wgt[0][..., None] * outs[0]
    for gi in range(1, N_GROUPS):
        o = o + wgt[gi][..., None] * outs[gi]
    y = o.astype(h.dtype).reshape(Bn, T, ATT_DIM) @ w_out
    return y, rows


def swiglu(h, wg, wu, wd):
    return (jax.nn.silu(h @ wg) * (h @ wu)) @ wd


def moe_ffn(h, w_router, b_router, w_gate, w_up, w_down):
    Bn, T, D = h.shape
    hf = h.reshape(Bn * T, D)
    logits = (hf @ w_router).astype(jnp.float32) + b_router.astype(jnp.float32)
    top_v, top_i = lax.top_k(logits, TOP_K)
    gates = jax.nn.softmax(top_v, axis=-1)
    combine = jnp.sum(jax.nn.one_hot(top_i, N_EXPERTS, dtype=jnp.float32) * gates[..., None], axis=1)
    out = combine[:, 0:1] * swiglu(hf, w_gate[0], w_up[0], w_down[0]).astype(jnp.float32)
    for e in range(1, N_EXPERTS):
        out = out + combine[:, e:e + 1] * swiglu(hf, w_gate[e], w_up[e], w_down[e]).astype(jnp.float32)
    return out.astype(h.dtype).reshape(Bn, T, D)


def trunk(x, c, pos, s0, conv0, kv_bufs, w_ada, b_ada, norm_mix, norm_ffn, norm_final,
          w_in_a, conv_w_a, a_log, dt_bias, norm_o_a, w_out_a, w_in_b, w_out_b,
          w_gate_d, w_up_d, w_down_d, w_router, b_router, w_gate_e, w_up_e, w_down_e):
    Bn = x.shape[0]
    mod = jnp.einsum('bd,ldk->lbk', jax.nn.silu(c), w_ada) + b_ada[:, None]
    mod = mod.reshape(DEPTH, Bn, 6, D_MODEL)
    new_s, new_conv = [], []
    new_kv = [[] for _ in range(N_GROUPS)]
    for i in range(DEPTH):
        j = i // 2
        sh_m, sc_m, gt_m, sh_f, sc_f, gt_f = [mod[i, :, n][:, None] for n in range(6)]
        h = rms_norm(x, norm_mix[i]) * (1 + sc_m) + sh_m
        if i % 2 == 0:
            y, cb, S = gdn_mixer(h, w_in_a[j], conv_w_a[j], a_log[j], dt_bias[j], norm_o_a[j],
                                 w_out_a[j], conv0[j], s0[j])
            new_conv.append(cb)
            new_s.append(S)
        else:
            bufs = None if kv_bufs is None else tuple(buf[j] for buf in kv_bufs)
            y, rows = dilated_mixer(h, pos, w_in_b[j], w_out_b[j], bufs)
            for gi in range(N_GROUPS):
                new_kv[gi].append(rows[gi])
        x = x + gt_m * y
        h = rms_norm(x, norm_ffn[i]) * (1 + sc_f) + sh_f
        if i % 2 == 0:
            y = swiglu(h, w_gate_d[j], w_up_d[j], w_down_d[j])
        else:
            y = moe_ffn(h, w_router[j], b_router[j], w_gate_e[j], w_up_e[j], w_down_e[j])
        x = x + gt_f * y
    y = rms_norm(x, norm_final)
    return (y, jnp.stack(new_s), jnp.stack(new_conv),
            jnp.stack(new_kv[0]), jnp.stack(new_kv[1]), jnp.stack(new_kv[2]))


def setup_inputs(seed: int = 0) -> dict:
    key = jax.random.key(seed)
    ks = jax.random.split(key, 32)
    f32 = jnp.float32

    def nrm(k, shape, scale):
        return jax.random.normal(k, shape, f32) * scale

    lg = [min(w, PAST_LEN) for w, _ in DIL_GROUPS]
    dt = jnp.exp(jax.random.uniform(ks[17], (N_A_LAYERS, GDN_HV), f32, math.log(1e-3), math.log(1e-1)))
    return {
        'x_prompt': nrm(ks[0], (BATCH, SEQ, D_MODEL), 1.0),
        'x_sample': nrm(ks[1], (DEC_BATCH, DEC_SEQ, D_MODEL), 1.0),
        'state_gdn': nrm(ks[2], (N_A_LAYERS, DEC_BATCH, GDN_HV, GDN_DK, GDN_DV), 0.05),
        'state_conv': nrm(ks[3], (N_A_LAYERS, DEC_BATCH, CONV_W - 1, GDN_QKV_DIM), 1.0),
        'cache_kv_w128': nrm(ks[4], (N_B_LAYERS, DEC_BATCH, lg[0], 2, ATT_H, ATT_DH), 1.0),
        'cache_kv_w512': nrm(ks[5], (N_B_LAYERS, DEC_BATCH, lg[1], 2, ATT_H, ATT_DH), 1.0),
        'cache_kv_w2048': nrm(ks[6], (N_B_LAYERS, DEC_BATCH, lg[2], 2, ATT_H, ATT_DH), 1.0),
        'c_prompt': nrm(ks[7], (BATCH, D_MODEL), 1.0),
        'c_sample': nrm(ks[8], (DEC_BATCH, D_MODEL), 1.0),
        'w_ada': nrm(ks[9], (DEPTH, D_MODEL, 6 * D_MODEL), 0.5 * D_MODEL ** -0.5),
        'b_ada': nrm(ks[10], (DEPTH, 6 * D_MODEL), 0.02),
        'norm_mix': 1.0 + nrm(ks[11], (DEPTH, D_MODEL), 0.02),
        'norm_ffn': 1.0 + nrm(ks[12], (DEPTH, D_MODEL), 0.02),
        'norm_final': 1.0 + nrm(ks[13], (D_MODEL,), 0.02),
        'w_in_a': nrm(ks[14], (N_A_LAYERS, D_MODEL, GDN_IN_DIM), D_MODEL ** -0.5),
        'conv_w_a': nrm(ks[15], (N_A_LAYERS, CONV_W, GDN_QKV_DIM), CONV_W ** -0.5),
        'a_log': jnp.log(jax.random.uniform(ks[16], (N_A_LAYERS, GDN_HV), f32, 1.0, 16.0)),
        'dt_bias': dt + jnp.log(-jnp.expm1(-dt)),
        'norm_o_a': 1.0 + nrm(ks[18], (N_A_LAYERS, GDN_DV), 0.02),
        'w_out_a': nrm(ks[19], (N_A_LAYERS, GDN_VAL_DIM, D_MODEL), GDN_VAL_DIM ** -0.5),
        'w_in_b': nrm(ks[20], (N_B_LAYERS, D_MODEL, ATT_IN_DIM), D_MODEL ** -0.5),
        'w_out_b': nrm(ks[21], (N_B_LAYERS, ATT_DIM, D_MODEL), ATT_DIM ** -0.5),
        'w_gate_d': nrm(ks[22], (N_A_LAYERS, D_MODEL, D_FF), D_MODEL ** -0.5),
        'w_up_d': nrm(ks[23], (N_A_LAYERS, D_MODEL, D_FF), D_MODEL ** -0.5),
        'w_down_d': nrm(ks[24], (N_A_LAYERS, D_FF, D_MODEL), D_FF ** -0.5),
        'w_router': nrm(ks[25], (N_B_LAYERS, D_MODEL, N_EXPERTS), D_MODEL ** -0.5),
        'b_router': nrm(ks[26], (N_B_LAYERS, N_EXPERTS), 0.01),
        'w_gate_e': nrm(ks[27], (N_B_LAYERS, N_EXPERTS, D_MODEL, D_FF), D_MODEL ** -0.5),
        'w_up_e': nrm(ks[28], (N_B_LAYERS, N_EXPERTS, D_MODEL, D_FF), D_MODEL ** -0.5),
        'w_down_e': nrm(ks[29], (N_B_LAYERS, N_EXPERTS, D_FF, D_MODEL), D_FF ** -0.5),
    }


def reference(x_prompt, x_sample, state_gdn, state_conv, cache_kv_w128, cache_kv_w512, cache_kv_w2048,
              c_prompt, c_sample, w_ada, b_ada, norm_mix, norm_ffn, norm_final,
              w_in_a, conv_w_a, a_log, dt_bias, norm_o_a, w_out_a, w_in_b, w_out_b,
              w_gate_d, w_up_d, w_down_d, w_router, b_router, w_gate_e, w_up_e, w_down_e):
    n_prompt = x_prompt.shape[0]
    s0_prompt = jnp.zeros((N_A_LAYERS, n_prompt, GDN_HV, GDN_DK, GDN_DV), state_gdn.dtype)
    conv0_prompt = jnp.zeros((N_A_LAYERS, n_prompt, CONV_W - 1, GDN_QKV_DIM), state_conv.dtype)
    y_p, s_p, cv_p, kv128_p, kv512_p, kv2048_p = trunk(
        x_prompt, c_prompt, jnp.arange(SEQ), s0_prompt, conv0_prompt, None,
        w_ada, b_ada, norm_mix, norm_ffn, norm_final, w_in_a, conv_w_a, a_log, dt_bias, norm_o_a,
        w_out_a, w_in_b, w_out_b, w_gate_d, w_up_d, w_down_d, w_router, b_router,
        w_gate_e, w_up_e, w_down_e)
    y_s, s_s, cv_s, kv128_s, kv512_s, kv2048_s = trunk(
        x_sample, c_sample, PAST_LEN + jnp.arange(DEC_SEQ), state_gdn, state_conv,
        (cache_kv_w128, cache_kv_w512, cache_kv_w2048),
        w_ada, b_ada, norm_mix, norm_ffn, norm_final, w_in_a, conv_w_a, a_log, dt_bias, norm_o_a,
        w_out_a, w_in_b, w_out_b, w_gate_d, w_up_d, w_down_d, w_router, b_router,
        w_gate_e, w_up_e, w_down_e)
    return (y_p, y_s, s_p, cv_p, kv128_p, kv512_p, kv2048_p, s_s, cv_s, kv128_s, kv512_s, kv2048_s)
```

```python
import functools
import math

import jax
import jax.numpy as jnp
from jax import lax
from jax.experimental import pallas as pl
from jax.experimental.pallas import tpu as pltpu

F32 = jnp.float32
BF16 = jnp.bfloat16

DEPTH = 4
GDN_HK = 8
GDN_HV = 16
GDN_DK = 128
GDN_DV = 128
GDN_KEY_DIM = GDN_HK * GDN_DK
GDN_VAL_DIM = GDN_HV * GDN_DV
GDN_QKV_DIM = 2 * GDN_KEY_DIM + GDN_VAL_DIM
CONV_W = 4
GDN_CHUNK = 64
ATT_H = 16
ATT_DH = 64
ATT_DIM = ATT_H * ATT_DH
DIL_GROUPS = ((128, 1), (512, 4), (2048, 16))
N_GROUPS = len(DIL_GROUPS)
ATT_BLOCK = 128
ROPE_THETA = 10000.0
N_EXPERTS = 8
EPS = 1e-6
NEG_INF = -1e30
PAST_LEN = 2048

V7X_VMEM_BYTES = 64 * 1024 * 1024
VMEM_LIMIT_BYTES = 52 * 1024 * 1024
VMEM_BLOCK_BUDGET = 30 * 1024 * 1024
LANES = 128


def _cparams(sem):
    return pltpu.CompilerParams(dimension_semantics=sem, vmem_limit_bytes=VMEM_LIMIT_BYTES)


def _dot(a, b):
    return jnp.dot(a.astype(BF16), b.astype(BF16), preferred_element_type=F32)


def _dot_nt(a, b):
    return lax.dot_general(a.astype(BF16), b.astype(BF16), (((1,), (1,)), ((), ())),
                           preferred_element_type=F32)


def _dot_tn(a, b):
    return lax.dot_general(a.astype(BF16), b.astype(BF16), (((0,), (0,)), ((), ())),
                           preferred_element_type=F32)


def _split2(a):
    hi = a.astype(BF16)
    lo = (a - hi.astype(F32)).astype(BF16)
    return hi, lo


def _split3(a):
    hi = a.astype(BF16)
    r = a - hi.astype(F32)
    mid = r.astype(BF16)
    lo = (r - mid.astype(F32)).astype(BF16)
    return hi, mid, lo


def _dot3(a, b):
    ah, al = _split2(a)
    bh, bl = _split2(b)
    return _dot(ah, bh) + (_dot(ah, bl) + _dot(al, bh))


def _dot_exact_lhs(c, b, passes=3):
    parts = _split3(b) if passes == 3 else _split2(b)
    acc = _dot(c, parts[0])
    for p in parts[1:]:
        acc = acc + _dot(c, p)
    return acc


def _dot_exact_rhs(a, c, passes=3):
    parts = _split3(a) if passes == 3 else _split2(a)
    acc = _dot(parts[0], c)
    for p in parts[1:]:
        acc = acc + _dot(p, c)
    return acc


def _silu(x):
    return x * jax.nn.sigmoid(x)


def _iota(shape, dim):
    return lax.broadcasted_iota(jnp.int32, shape, dim)


def _norm_mod_body(x_ref, nw_ref, *rest, modulate):
    if modulate:
        sh_ref, sc_ref, o_ref = rest
    else:
        (o_ref,) = rest
    x = x_ref[...].astype(F32)
    y = x * lax.rsqrt(jnp.mean(x * x, axis=-1, keepdims=True) + EPS) * nw_ref[...]
    if modulate:
        y = y * (1.0 + sc_ref[...]) + sh_ref[...]
    o_ref[...] = y.astype(o_ref.dtype)


def _norm_mod(x, nw, shift, scale, rows_per_group, out_dtype):
    M, D = x.shape
    tm = min(M, 1024)
    assert M % tm == 0
    modulate = shift is not None
    in_specs = [pl.BlockSpec((tm, D), lambda i: (i, 0)), pl.BlockSpec((1, D), lambda i: (0, 0))]
    args = [x, nw.reshape(1, D)]
    if modulate:
        if rows_per_group == 1:
            spec = pl.BlockSpec((tm, D), lambda i: (i, 0))
            args += [shift, scale]
        else:
            assert rows_per_group % tm == 0
            spec = pl.BlockSpec((None, 1, D), lambda i: ((i * tm) // rows_per_group, 0, 0))
            args += [shift.reshape(-1, 1, D), scale.reshape(-1, 1, D)]
        in_specs += [spec, spec]
    return pl.pallas_call(
        functools.partial(_norm_mod_body, modulate=modulate),
        out_shape=jax.ShapeDtypeStruct((M, D), out_dtype),
        grid=(M // tm,),
        in_specs=in_specs,
        out_specs=pl.BlockSpec((tm, D), lambda i: (i, 0)),
        compiler_params=_cparams(("parallel",)),
        name="norm_mod",
    )(*args)


def _mm_body(*refs, pre_silu, swiglu, has_rowscale, rowscale_col, has_res, has_gate, has_res2):
    it = iter(refs)
    x_ref = next(it)
    w_ref = next(it)
    w2_ref = next(it) if swiglu else None
    rs_ref = next(it) if has_rowscale else None
    res_ref = next(it) if has_res else None
    gate_ref = next(it) if has_gate else None
    res2_ref = next(it) if has_res2 else None
    o_ref = next(it)
    x = x_ref[...]
    if pre_silu:
        x = _silu(x.astype(F32))
    xb = x.astype(BF16)
    acc = jnp.dot(xb, w_ref[...].astype(BF16), preferred_element_type=F32)
    if swiglu:
        acc = _silu(acc) * jnp.dot(xb, w2_ref[...].astype(BF16), preferred_element_type=F32)
    if has_rowscale:
        acc = acc * rs_ref[...][:, rowscale_col:rowscale_col + 1]
    if has_res:
        acc = acc + res_ref[...]
    if has_gate:
        acc = acc * gate_ref[...]
    if has_res2:
        acc = acc + res2_ref[...]
    o_ref[...] = acc.astype(o_ref.dtype)


def _pick_tiles(M, K, N, x_bytes, n_w, out_bytes, n_res):
    tm = min(M, 1024)
    while M % tm:
        tm //= 2
    cands = [t for t in (1024, 896, 768, 512, 384, 256, 128) if N % t == 0]
    if not cands:
        cands = [N]

    def est(tm_, tn_):
        blocks = tm_ * K * x_bytes + n_w * K * tn_ * 4 + tm_ * tn_ * (out_bytes + 4 * n_res)
        temps = n_w * K * tn_ * 2 + tm_ * K * 2 + (1 + n_w) * tm_ * tn_ * 4
        return 2 * blocks + temps

    ci = 0
    while est(tm, cands[ci]) > VMEM_BLOCK_BUDGET:
        if ci + 1 < len(cands) and cands[ci] > 256:
            ci += 1
        elif tm > 256 and M % (tm // 2) == 0:
            tm //= 2
        elif ci + 1 < len(cands):
            ci += 1
        else:
            break
    return tm, cands[ci]


def _mm(x, w, lead, n_cols, col0=0, *, out_dtype, w2=None, pre_silu=False, rowscale=None,
        rowscale_col=0, res=None, gate=None, res2=None, rows_per_group=1, name="mm"):
    M, K = x.shape
    swiglu = w2 is not None
    n_res = int(res is not None) + int(res2 is not None) + int(gate is not None)
    tm, tn = _pick_tiles(M, K, n_cols, x.dtype.itemsize, 2 if swiglu else 1,
                         jnp.dtype(out_dtype).itemsize, n_res)
    assert col0 % tn == 0 and n_cols % tn == 0
    nj, ni = n_cols // tn, M // tm
    jb0 = col0 // tn
    nlead = len(lead)
    wblock = (None,) * nlead + (K, tn)

    def wmap(j, i):
        return tuple(lead) + (0, jb0 + j)

    in_specs = [pl.BlockSpec((tm, K), lambda j, i: (i, 0)), pl.BlockSpec(wblock, wmap)]
    args = [x, w]
    if swiglu:
        in_specs.append(pl.BlockSpec(wblock, wmap))
        args.append(w2)
    if rowscale is not None:
        in_specs.append(pl.BlockSpec((tm, rowscale.shape[1]), lambda j, i: (i, 0)))
        args.append(rowscale)

    def tile_spec(a):
        if a.shape[0] == M:
            return pl.BlockSpec((tm, tn), lambda j, i: (i, j)), a
        assert rows_per_group % tm == 0 and a.shape[0] * rows_per_group == M
        return (pl.BlockSpec((None, 1, tn), lambda j, i: ((i * tm) // rows_per_group, 0, j)),
                a.reshape(a.shape[0], 1, a.shape[1]))

    for a in (res, gate, res2):
        if a is not None:
            s, a2 = tile_spec(a)
            in_specs.append(s)
            args.append(a2)
    body = functools.partial(
        _mm_body, pre_silu=pre_silu, swiglu=swiglu, has_rowscale=rowscale is not None,
        rowscale_col=rowscale_col, has_res=res is not None, has_gate=gate is not None,
        has_res2=res2 is not None)
    return pl.pallas_call(
        body,
        out_shape=jax.ShapeDtypeStruct((M, n_cols), out_dtype),
        grid=(nj, ni),
        in_specs=in_specs,
        out_specs=pl.BlockSpec((tm, tn), lambda j, i: (i, j)),
        compiler_params=_cparams(("parallel", "parallel")),
        name=name,
    )(*args)


def _mm_small(x, w_small, name):
    M, K = x.shape
    n = w_small.shape[1]
    tm = min(M, 1024)

    def body(x_ref, w_ref, o_ref):
        o_ref[...] = _dot(x_ref[...], w_ref[...])

    return pl.pallas_call(
        body,
        out_shape=jax.ShapeDtypeStruct((M, n), F32),
        grid=(M // tm,),
        in_specs=[pl.BlockSpec((tm, K), lambda i: (i, 0)), pl.BlockSpec((K, n), lambda i: (0, 0))],
        out_specs=pl.BlockSpec((tm, n), lambda i: (i, 0)),
        compiler_params=_cparams(("parallel",)),
        name=name,
    )(x, w_small)


GDN_PREP_ROWS = 256
GDN_PREP_CHUNKS = GDN_PREP_ROWS // GDN_CHUNK
PAIR = 2 * GDN_CHUNK


def _gdn_gates(ab, alog, dtb):
    a = ab[:, :GDN_HV]
    b = ab[:, GDN_HV:]
    xg = a + dtb
    softplus = jnp.maximum(xg, 0.0) + jnp.log1p(jnp.exp(-jnp.abs(xg)))
    return -jnp.exp(alog) * softplus, jax.nn.sigmoid(b)


def _gdn_prep_body(q_ref, qp_ref, k_ref, kp_ref, v_ref, vp_ref, cwq_ref, cwk_ref, cwv_ref,
                   ab_ref, alog_ref, dtb_ref,
                   u_ref, w_ref, qd_ref, kd_ref, qk_ref, gl_ref):
    i = pl.program_id(1)
    hk = pl.program_id(2)
    rows = GDN_PREP_ROWS
    first = i == 0

    def conv(x_ref, xp_ref, cw_ref):
        cur = x_ref[...]
        prev = jnp.where(first, 0.0, xp_ref[...])
        ext = jnp.concatenate([prev, cur], axis=0)
        cw = cw_ref[...]
        y = cur * cw[CONV_W - 1:CONV_W, :]
        for s in range(1, CONV_W):
            y = y + ext[8 - s:8 - s + rows, :] * cw[CONV_W - 1 - s:CONV_W - s, :]
        return _silu(y)

    qc = conv(q_ref, qp_ref, cwq_ref)
    kc = conv(k_ref, kp_ref, cwk_ref)
    vc = conv(v_ref, vp_ref, cwv_ref)
    qn = qc * lax.rsqrt(jnp.sum(qc * qc, axis=-1, keepdims=True) + EPS) * (GDN_DK ** -0.5)
    kn = kc * lax.rsqrt(jnp.sum(kc * kc, axis=-1, keepdims=True) + EPS)

    g16, be16 = _gdn_gates(ab_ref[...], alog_ref[...], dtb_ref[...])
    r16 = _iota((GDN_HV, LANES), 0)

    def head_bcast(x16, hv):
        return _dot_exact_rhs(x16, (r16 == hv).astype(BF16))

    g_b = [head_bcast(g16, 2 * hk + s) for s in (0, 1)]
    be_b = [head_bcast(be16, 2 * hk + s) for s in (0, 1)]

    ri = _iota((PAIR, PAIR), 0)
    ci = _iota((PAIR, PAIR), 1)
    same = (ri >> 6) == (ci >> 6)
    causal = same & (ri >= ci)
    strict = same & (ri > ci)
    blk16 = (ri >> 4) == (ci >> 4)
    tri = causal.astype(BF16)
    eye = (ri == ci).astype(F32)
    r8 = _iota((8, LANES), 0)

    for c in range(GDN_PREP_CHUNKS):
        sl = slice(c * GDN_CHUNK, (c + 1) * GDN_CHUNK)
        k2 = jnp.concatenate([kn[sl], kn[sl]], axis=0)
        q2 = jnp.concatenate([qn[sl], qn[sl]], axis=0)
        v2 = jnp.concatenate([vc[sl, :GDN_DV], vc[sl, GDN_DV:]], axis=0)
        g2 = jnp.concatenate([g_b[0][sl], g_b[1][sl]], axis=0)
        b2 = jnp.concatenate([be_b[0][sl], be_b[1][sl]], axis=0)

        res = _dot_exact_lhs(tri, jnp.concatenate([jnp.where(strict, g2, 0.0), g2], axis=1))
        diff = res[:, :PAIR]
        gcum = res[:, PAIR:]
        decay = jnp.where(causal, jnp.exp(jnp.where(causal, diff, 0.0)), 0.0)
        eg = jnp.exp(gcum)
        gl0 = gcum[GDN_CHUNK - 1:GDN_CHUNK, :]
        gl1 = gcum[PAIR - 1:PAIR, :]
        glast = jnp.where(ri < GDN_CHUNK, gl0, gl1)

        kk = _dot_nt(k2, k2)
        qkm = _dot_nt(q2, k2) * decay
        lmat = jnp.where(strict, b2 * kk * decay, 0.0)

        ld = jnp.where(blk16, lmat, 0.0)
        nn = jnp.where(blk16, 0.0, lmat)
        x = eye - ld
        p = _dot3(ld, ld)
        x = x + _dot3(x, p)
        p = _dot3(p, p)
        x = x + _dot3(x, p)
        p = _dot3(p, p)
        x = x + _dot3(x, p)
        mm = _dot3(x, nn)
        m2 = _dot3(mm, mm)
        rhs = jnp.concatenate([b2 * v2, b2 * eg * k2], axis=1)
        y = _dot3(x, rhs)
        y = y + _dot3(m2, y)
        y = y - _dot3(mm, y)

        u_ref[c] = y[:, :GDN_DV]
        w_ref[c] = y[:, GDN_DV:].astype(BF16)
        qd_ref[c] = (q2 * eg).astype(BF16)
        kd_ref[c] = (k2 * jnp.exp(glast - gcum)).astype(BF16)
        qk_ref[c] = qkm.astype(BF16)
        gl_ref[c] = jnp.where(r8 == 0, jnp.exp(gl0), jnp.where(r8 == 1, jnp.exp(gl1), 0.0))


def _gdn_prep(proj, ab, conv_w, a_log, dt_bias, layer, Bn, T):
    rows = GDN_PREP_ROWS
    nblk = T // rows
    nch = T // GDN_CHUNK
    cpb = GDN_PREP_CHUNKS

    def cur(width, col):
        return pl.BlockSpec((rows, width), lambda b, i, h: (b * nblk + i, col(h)))

    def prev(width, col):
        return pl.BlockSpec((8, width),
                            lambda b, i, h: (jnp.maximum((b * T + i * rows) // 8 - 1, 0), col(h)))

    def cw(width, col):
        return pl.BlockSpec((None, CONV_W, width), lambda b, i, h: (layer, 0, col(h)))

    qcol = lambda h: h
    kcol = lambda h: GDN_HK + h
    vcol = lambda h: GDN_HK + h
    in_specs = [cur(128, qcol), prev(128, qcol), cur(128, kcol), prev(128, kcol),
                cur(256, vcol), prev(256, vcol),
                cw(128, qcol), cw(128, kcol), cw(256, vcol),
                pl.BlockSpec((rows, 2 * GDN_HV), lambda b, i, h: (b * nblk + i, 0)),
                pl.BlockSpec((None, 1, GDN_HV), lambda b, i, h: (layer, 0, 0)),
                pl.BlockSpec((None, 1, GDN_HV), lambda b, i, h: (layer, 0, 0))]
    big = lambda dt: jax.ShapeDtypeStruct((Bn, GDN_HK, nch, PAIR, LANES), dt)
    ospec = pl.BlockSpec((None, None, cpb, PAIR, LANES), lambda b, i, h: (b, h, i, 0, 0))
    out_shape = (big(F32), big(BF16), big(BF16), big(BF16), big(BF16),
                 jax.ShapeDtypeStruct((Bn, GDN_HK, nch, 8, LANES), F32))
    out_specs = (ospec, ospec, ospec, ospec, ospec,
                 pl.BlockSpec((None, None, cpb, 8, LANES), lambda b, i, h: (b, h, i, 0, 0)))
    return pl.pallas_call(
        _gdn_prep_body,
        out_shape=out_shape,
        grid=(Bn, nblk, GDN_HK),
        in_specs=in_specs,
        out_specs=out_specs,
        compiler_params=_cparams(("parallel", "parallel", "parallel")),
        name="gdn_prep",
    )(proj, proj, proj, proj, proj, proj, conv_w, conv_w, conv_w, ab,
      a_log.reshape(-1, 1, GDN_HV), dt_bias.reshape(-1, 1, GDN_HV))


GDN_SCAN_ROWS = 256
GDN_SCAN_CHUNKS = GDN_SCAN_ROWS // GDN_CHUNK


def _gated_norm_store(o_scr, z_ref, no_ref, og_ref):
    for h in range(GDN_HV):
        sl = slice(h * GDN_DV, (h + 1) * GDN_DV)
        oh = o_scr[:, sl]
        zh = z_ref[:, sl]
        on = oh * lax.rsqrt(jnp.mean(oh * oh, axis=-1, keepdims=True) + EPS) * no_ref[...]
        og_ref[:, sl] = (on * _silu(zh)).astype(og_ref.dtype)


def _gdn_scan_body(u_ref, w_ref, qd_ref, kd_ref, qk_ref, gl_ref, z_ref, no_ref,
                   og_ref, s_ref, o_scr):
    @pl.when(pl.program_id(1) == 0)
    def _():
        s_ref[...] = jnp.zeros_like(s_ref)

    C = GDN_CHUNK
    for c in range(GDN_SCAN_CHUNKS):
        for hk in range(GDN_HK):
            u2 = u_ref[hk, c]
            w2 = w_ref[hk, c]
            qd2 = qd_ref[hk, c]
            kd2 = kd_ref[hk, c]
            qk2 = qk_ref[hk, c]
            gl = gl_ref[hk, c]
            s0 = s_ref[2 * hk]
            s1 = s_ref[2 * hk + 1]
            vn0 = u2[:C] - _dot(w2[:C], s0)
            vn1 = u2[C:] - _dot(w2[C:], s1)
            oi = _dot(qk2, jnp.concatenate([vn0, vn1], axis=0))
            o0 = _dot(qd2[:C], s0) + oi[:C]
            o1 = _dot(qd2[C:], s1) + oi[C:]
            s_ref[2 * hk] = s0 * gl[0:1, :] + _dot_tn(kd2[:C], vn0)
            s_ref[2 * hk + 1] = s1 * gl[1:2, :] + _dot_tn(kd2[C:], vn1)
            o_scr[c * C:(c + 1) * C, (2 * hk) * GDN_DV:(2 * hk + 1) * GDN_DV] = o0
            o_scr[c * C:(c + 1) * C, (2 * hk + 1) * GDN_DV:(2 * hk + 2) * GDN_DV] = o1
    _gated_norm_store(o_scr, z_ref, no_ref, og_ref)


def _gdn_scan(prep, proj, norm_o, layer, Bn, T):
    u, w, qd, kd, qk, gl = prep
    rows = GDN_SCAN_ROWS
    nblk = T // rows
    cpb = GDN_SCAN_CHUNKS
    bspec = pl.BlockSpec((None, GDN_HK, cpb, PAIR, LANES), lambda b, i: (b, 0, i, 0, 0))
    in_specs = [bspec, bspec, bspec, bspec, bspec,
                pl.BlockSpec((None, GDN_HK, cpb, 8, LANES), lambda b, i: (b, 0, i, 0, 0)),
                pl.BlockSpec((rows, GDN_VAL_DIM), lambda b, i: (b * nblk + i, GDN_QKV_DIM // GDN_VAL_DIM)),
                pl.BlockSpec((None, 1, GDN_DV), lambda b, i: (layer, 0, 0))]
    out_shape = (jax.ShapeDtypeStruct((Bn * T, GDN_VAL_DIM), BF16),
                 jax.ShapeDtypeStruct((Bn, GDN_HV, GDN_DK, GDN_DV), F32))
    out_specs = (pl.BlockSpec((rows, GDN_VAL_DIM), lambda b, i: (b * nblk + i, 0)),
                 pl.BlockSpec((None, GDN_HV, GDN_DK, GDN_DV), lambda b, i: (b, 0, 0, 0)))
    return pl.pallas_call(
        _gdn_scan_body,
        out_shape=out_shape,
        grid=(Bn, nblk),
        in_specs=in_specs,
        out_specs=out_specs,
        scratch_shapes=[pltpu.VMEM((rows, GDN_VAL_DIM), F32)],
        compiler_params=_cparams(("parallel", "arbitrary")),
        name="gdn_scan",
    )(u, w, qd, kd, qk, gl, proj, norm_o.reshape(-1, 1, GDN_DV))


GDN_STEP_ROWS = 8


def _gdn_step_body(proj_ref, ab_ref, cb_ref, cw_ref, alog_ref, dtb_ref, no_ref, st_ref,
                   og_ref, cbo_ref, sto_ref, o_scr):
    R = GDN_STEP_ROWS
    qkv = proj_ref[:, :GDN_QKV_DIM]
    cw = cw_ref[...]
    y = cb_ref[0] * cw[0:1, :]
    y = y + cb_ref[1] * cw[1:2, :]
    y = y + cb_ref[2] * cw[2:3, :]
    y = y + qkv * cw[3:4, :]
    qkv_c = _silu(y)
    cbo_ref[0] = cb_ref[1]
    cbo_ref[1] = cb_ref[2]
    cbo_ref[2] = qkv

    g16, be16 = _gdn_gates(ab_ref[...], alog_ref[...], dtb_ref[...])
    expand = ((_iota((GDN_HV, GDN_VAL_DIM), 1) >> 7) == _iota((GDN_HV, GDN_VAL_DIM), 0)).astype(BF16)
    eg_f = jnp.exp(_dot_exact_rhs(g16, expand))
    be_f = _dot_exact_rhs(be16, expand)
    r8 = _iota((8, LANES), 0)

    for hk in range(GDN_HK):
        qh = qkv_c[:, hk * GDN_DK:(hk + 1) * GDN_DK]
        kh = qkv_c[:, GDN_KEY_DIM + hk * GDN_DK:GDN_KEY_DIM + (hk + 1) * GDN_DK]
        qn = qh * lax.rsqrt(jnp.sum(qh * qh, axis=-1, keepdims=True) + EPS) * (GDN_DK ** -0.5)
        kn = kh * lax.rsqrt(jnp.sum(kh * kh, axis=-1, keepdims=True) + EPS)
        qk = jnp.sum(qn * kn, axis=-1, keepdims=True)
        for s in (0, 1):
            hv = 2 * hk + s
            sl = slice(hv * GDN_DV, (hv + 1) * GDN_DV)
            eg = eg_f[:, sl]
            be = be_f[:, sl]
            vh = qkv_c[:, 2 * GDN_KEY_DIM + hv * GDN_DV:2 * GDN_KEY_DIM + (hv + 1) * GDN_DV]
            w_rows = be * eg * kn
            qd_rows = qn * eg
            for bb in range(R):
                st = st_ref[bb, hv]
                lhs = jnp.where(r8 == 0, w_rows[bb:bb + 1], jnp.where(r8 == 1, qd_rows[bb:bb + 1], 0.0))
                rr = _dot(lhs, st)
                vn = be[bb:bb + 1] * vh[bb:bb + 1] - rr[0:1]
                o = rr[1:2] + qk[bb:bb + 1] * vn
                kpad = jnp.where(r8 == 0, kn[bb:bb + 1], 0.0)
                vpad = jnp.where(r8 == 0, vn, 0.0)
                sto_ref[bb, hv] = st * eg[bb:bb + 1] + _dot_tn(kpad, vpad)
                o_scr[bb:bb + 1, sl] = o
    _gated_norm_store(o_scr, proj_ref.at[:, GDN_QKV_DIM:], no_ref, og_ref)


def _gdn_step(proj, ab, conv_buf_t, conv_w, a_log, dt_bias, norm_o, state, layer):
    Bs = proj.shape[0]
    R = GDN_STEP_ROWS
    vec = lambda n: pl.BlockSpec((None, 1, n), lambda i: (layer, 0, 0))
    in_specs = [pl.BlockSpec((R, proj.shape[1]), lambda i: (i, 0)),
                pl.BlockSpec((R, 2 * GDN_HV), lambda i: (i, 0)),
                pl.BlockSpec((None, CONV_W - 1, R, GDN_QKV_DIM), lambda i: (layer, 0, i, 0)),
                pl.BlockSpec((None, CONV_W, GDN_QKV_DIM), lambda i: (layer, 0, 0)),
                vec(GDN_HV), vec(GDN_HV), vec(GDN_DV),
                pl.BlockSpec((None, R, GDN_HV, GDN_DK, GDN_DV), lambda i: (layer, i, 0, 0, 0))]
    out_shape = (jax.ShapeDtypeStruct((Bs, GDN_VAL_DIM), F32),
                 jax.ShapeDtypeStruct((CONV_W - 1, Bs, GDN_QKV_DIM), F32),
                 jax.ShapeDtypeStruct((Bs, GDN_HV, GDN_DK, GDN_DV), F32))
    out_specs = (pl.BlockSpec((R, GDN_VAL_DIM), lambda i: (i, 0)),
                 pl.BlockSpec((CONV_W - 1, R, GDN_QKV_DIM), lambda i: (0, i, 0)),
                 pl.BlockSpec((R, GDN_HV, GDN_DK, GDN_DV), lambda i: (i, 0, 0, 0)))
    return pl.pallas_call(
        _gdn_step_body,
        out_shape=out_shape,
        grid=(Bs // R,),
        in_specs=in_specs,
        out_specs=out_specs,
        scratch_shapes=[pltpu.VMEM((R, GDN_VAL_DIM), F32)],
        compiler_params=_cparams(("parallel",)),
        name="gdn_step",
    )(proj, ab, conv_buf_t, conv_w, a_log.reshape(-1, 1, GDN_HV), dt_bias.reshape(-1, 1, GDN_HV),
      norm_o.reshape(-1, 1, GDN_DV), state)


def _rope_tables(ang):
    lane = _iota(ang.shape, 1)
    cs = jnp.cos(ang)
    sn = jnp.sin(ang)
    sg = jnp.where((lane & (ATT_DH - 1)) < ATT_DH // 2, -sn, sn)
    reps = ATT_DIM // LANES
    return jnp.concatenate([cs] * reps, axis=1), jnp.concatenate([sg] * reps, axis=1)


def _rope_apply(x, cs, sg):
    lane = _iota(x.shape, 1)
    first_half = (lane & (ATT_DH - 1)) < ATT_DH // 2
    half = ATT_DH // 2
    partner = jnp.where(first_half, pltpu.roll(x, x.shape[1] - half, 1), pltpu.roll(x, half, 1))
    return x * cs + partner * sg


def _rope_body(invf_ref, x_ref, o_ref, *, pos0, pos_step, row0, tr, col0):
    i = pl.program_id(1)
    c = pl.program_id(2)
    is_v = lax.rem(col0 + c, 3) == 2

    @pl.when(is_v)
    def _():
        o_ref[...] = x_ref[...].astype(o_ref.dtype)

    @pl.when(jnp.logical_not(is_v))
    def _():
        pos = (pos0 + pos_step * (row0 + i * tr + _iota((tr, LANES), 0))).astype(F32)
        cs, sg = _rope_tables(pos * invf_ref[...])
        o_ref[...] = _rope_apply(x_ref[...], cs, sg).astype(o_ref.dtype)


def _rope(proj, invf, Bn, T, row0, nrows, col0, ncols, out_dtype, pos0=0, pos_step=1):
    tr = min(nrows, 256)
    assert nrows % tr == 0 and row0 % tr == 0 and T % tr == 0
    nb = nrows // tr
    body = functools.partial(_rope_body, pos0=pos0, pos_step=pos_step, row0=row0, tr=tr, col0=col0)
    return pl.pallas_call(
        body,
        out_shape=jax.ShapeDtypeStruct((Bn * nrows, ncols * ATT_DIM), out_dtype),
        grid=(Bn, nb, ncols),
        in_specs=[pl.BlockSpec((1, LANES), lambda b, i, c: (0, 0)),
                  pl.BlockSpec((tr, ATT_DIM), lambda b, i, c: (b * (T // tr) + row0 // tr + i, col0 + c))],
        out_specs=pl.BlockSpec((tr, ATT_DIM), lambda b, i, c: (b * nb + i, c)),
        compiler_params=_cparams(("parallel", "parallel", "parallel")),
        name="rope",
    )(invf, proj)


def _attn_body(q_ref, kp_ref, kc_ref, vp_ref, vc_ref, o_ref, lse_ref):
    n = pl.program_id(2)
    blk = ATT_BLOCK
    qi = _iota((blk, 2 * blk), 0)
    kj = _iota((blk, 2 * blk), 1)
    dist = qi + blk - kj
    n_win = blk
    mask = (dist >= 0) & (dist <= n_win) & ((kj >= blk) | (n > 0))
    lane = _iota((blk, LANES), 1)
    lo = lane < ATT_DH
    lse_acc = jnp.zeros((blk, LANES), F32)
    scale = ATT_DH ** -0.5
    for hp in range(ATT_H // 2):
        sl = slice(hp * LANES, (hp + 1) * LANES)
        q2 = q_ref[:, sl].astype(F32)
        k2 = jnp.concatenate([kp_ref[:, sl], kc_ref[:, sl]], axis=0)
        v2 = jnp.concatenate([vp_ref[:, sl], vc_ref[:, sl]], axis=0)
        outs = []
        for s in (0, 1):
            qm = jnp.where(lo if s == 0 else jnp.logical_not(lo), q2, 0.0)
            sc = _dot_nt(qm, k2) * scale
            sc = jnp.where(mask, sc, NEG_INF)
            m = jnp.max(sc, axis=-1, keepdims=True)
            p = jnp.exp(sc - m)
            l = jnp.sum(p, axis=-1, keepdims=True)
            outs.append(_dot(p, v2) * (1.0 / l))
            lse_acc = jnp.where(lane == 2 * hp + s, m + jnp.log(l), lse_acc)
        o_ref[:, sl] = jnp.where(lo, outs[0], outs[1])
    lse_ref[...] = lse_acc


def _attn_prompt(projr, gi, Bn, T):
    dil = DIL_GROUPS[gi][1]
    Td = T // dil
    nb = Td // ATT_BLOCK
    ncb = 3 * N_GROUPS
    pv = projr.reshape(Bn * Td, dil * ncb * ATT_DIM)

    def spec(which, prev):
        def imap(b, r, n):
            nn = jnp.maximum(n - 1, 0) if prev else n
            return (b * nb + nn, r * ncb + 3 * gi + which)
        return pl.BlockSpec((ATT_BLOCK, ATT_DIM), imap)

    o, lse = pl.pallas_call(
        _attn_body,
        out_shape=(jax.ShapeDtypeStruct((Bn * Td, dil * ATT_DIM), F32),
                   jax.ShapeDtypeStruct((Bn * Td, dil * LANES), F32)),
        grid=(Bn, dil, nb),
        in_specs=[spec(0, False), spec(1, True), spec(1, False), spec(2, True), spec(2, False)],
        out_specs=(pl.BlockSpec((ATT_BLOCK, ATT_DIM), lambda b, r, n: (b * nb + n, r)),
                   pl.BlockSpec((ATT_BLOCK, LANES), lambda b, r, n: (b * nb + n, r))),
        compiler_params=_cparams(("parallel", "parallel", "parallel")),
        name="attn_prompt",
    )(pv, pv, pv, pv, pv)
    return o.reshape(Bn * T, ATT_DIM), lse.reshape(Bn * T, LANES)


def _merge_body(o0_ref, o1_ref, o2_ref, l0_ref, l1_ref, l2_ref, out_ref):
    l0, l1, l2 = l0_ref[...], l1_ref[...], l2_ref[...]
    mx = jnp.maximum(jnp.maximum(l0, l1), l2)
    e0, e1, e2 = jnp.exp(l0 - mx), jnp.exp(l1 - mx), jnp.exp(l2 - mx)
    inv = 1.0 / (e0 + e1 + e2)
    expand = ((_iota((LANES, ATT_DIM), 1) >> 6) == _iota((LANES, ATT_DIM), 0)).astype(BF16)
    acc = _dot_exact_rhs(e0 * inv, expand, passes=2) * o0_ref[...]
    acc = acc + _dot_exact_rhs(e1 * inv, expand, passes=2) * o1_ref[...]
    acc = acc + _dot_exact_rhs(e2 * inv, expand, passes=2) * o2_ref[...]
    out_ref[...] = acc.astype(out_ref.dtype)


def _merge_groups(os, lses, out_dtype):
    M = os[0].shape[0]
    tm = min(M, 512)
    ospec = pl.BlockSpec((tm, ATT_DIM), lambda i: (i, 0))
    lspec = pl.BlockSpec((tm, LANES), lambda i: (i, 0))
    return pl.pallas_call(
        _merge_body,
        out_shape=jax.ShapeDtypeStruct((M, ATT_DIM), out_dtype),
        grid=(M // tm,),
        in_specs=[ospec, ospec, ospec, lspec, lspec, lspec],
        out_specs=ospec,
        compiler_params=_cparams(("parallel",)),
        name="attn_merge",
    )(*os, *lses)


ATT_STEP_ROWS = 4


def _attn_step_body(q_ref, k_ref, v_ref, c_ref, o_ref, lse_ref):
    ones = jnp.ones((ATT_DH, ATT_DH), BF16)
    scale = ATT_DH ** -0.5
    nkeys = c_ref.shape[1]
    for bb in range(ATT_STEP_ROWS):
        q = q_ref[bb]
        kc = c_ref[bb, :, 0]
        vc = c_ref[bb, :, 1]
        prod = (kc * q[None]).reshape(nkeys * ATT_H, ATT_DH)
        s = (_dot_exact_rhs(prod, ones, passes=2) * scale).reshape(nkeys, ATT_H, ATT_DH)
        sn = _dot_exact_rhs(q * k_ref[bb], ones, passes=2) * scale
        m = jnp.maximum(jnp.max(s, axis=0), sn)
        p = jnp.exp(s - m[None])
        pn = jnp.exp(sn - m)
        l = jnp.sum(p, axis=0) + pn
        o_un = jnp.sum(p * vc, axis=0) + pn * v_ref[bb]
        o_ref[bb] = o_un * (1.0 / l)
        lse_ref[bb] = m + jnp.log(l)


def _attn_step(q3, k3, v3, cache, layer, gi):
    Bs = q3.shape[0]
    dil = DIL_GROUPS[gi][1]
    L = cache.shape[2]
    nkeys = L // dil
    assert nkeys * dil == L and nkeys == DIL_GROUPS[gi][0] // dil
    cv = cache.reshape(cache.shape[0], Bs, nkeys, dil, 2, ATT_H, ATT_DH)
    R = ATT_STEP_ROWS
    rspec = pl.BlockSpec((R, ATT_H, ATT_DH), lambda i: (i, 0, 0))
    return pl.pallas_call(
        _attn_step_body,
        out_shape=(jax.ShapeDtypeStruct((Bs, ATT_H, ATT_DH), F32),
                   jax.ShapeDtypeStruct((Bs, ATT_H, ATT_DH), F32)),
        grid=(Bs // R,),
        in_specs=[rspec, rspec, rspec,
                  pl.BlockSpec((None, R, nkeys, None, 2, ATT_H, ATT_DH), lambda i: (layer, i, 0, 0, 0, 0, 0))],
        out_specs=(rspec, rspec),
        compiler_params=_cparams(("parallel",)),
        name="attn_step",
    )(q3, k3, v3, cv)


def _merge_step_body(o0_ref, o1_ref, o2_ref, l0_ref, l1_ref, l2_ref, out_ref):
    l0, l1, l2 = l0_ref[...], l1_ref[...], l2_ref[...]
    mx = jnp.maximum(jnp.maximum(l0, l1), l2)
    e0, e1, e2 = jnp.exp(l0 - mx), jnp.exp(l1 - mx), jnp.exp(l2 - mx)
    inv = 1.0 / (e0 + e1 + e2)
    out_ref[...] = (e0 * inv) * o0_ref[...] + (e1 * inv) * o1_ref[...] + (e2 * inv) * o2_ref[...]


def _merge_step(os, lses):
    shape = os[0].shape
    spec = pl.BlockSpec(shape, lambda i: (0, 0, 0))
    return pl.pallas_call(
        _merge_step_body,
        out_shape=jax.ShapeDtypeStruct(shape, F32),
        grid=(1,),
        in_specs=[spec] * 6,
        out_specs=spec,
        compiler_params=_cparams(("arbitrary",)),
        name="attn_merge_step",
    )(*os, *lses)


def _router_body(h_ref, w_ref, b_ref, o_ref):
    logits = _dot3(h_ref[...].astype(F32), w_ref[...]) + b_ref[...]
    lane = _iota(logits.shape, 1)
    logits = jnp.where(lane < N_EXPERTS, logits, -jnp.inf)
    m1 = jnp.max(logits, axis=-1, keepdims=True)
    i1 = jnp.min(jnp.where(logits == m1, lane, LANES), axis=-1, keepdims=True)
    rest = jnp.where(lane == i1, -jnp.inf, logits)
    m2 = jnp.max(rest, axis=-1, keepdims=True)
    i2 = jnp.min(jnp.where(rest == m2, lane, LANES), axis=-1, keepdims=True)
    e2 = jnp.exp(m2 - m1)
    g1 = 1.0 / (1.0 + e2)
    g2 = e2 * g1
    o_ref[...] = jnp.where(lane == i1, g1, 0.0) + jnp.where(lane == i2, g2, 0.0)


def _router(h, w_router, b_router):
    M, D = h.shape
    tm = min(M, 1024)
    wp = jnp.pad(w_router, ((0, 0), (0, LANES - N_EXPERTS)))
    bp = jnp.pad(b_router, (0, LANES - N_EXPERTS)).reshape(1, LANES)
    return pl.pallas_call(
        _router_body,
        out_shape=jax.ShapeDtypeStruct((M, LANES), F32),
        grid=(M // tm,),
        in_specs=[pl.BlockSpec((tm, D), lambda i: (i, 0)),
                  pl.BlockSpec((D, LANES), lambda i: (0, 0)),
                  pl.BlockSpec((1, LANES), lambda i: (0, 0))],
        out_specs=pl.BlockSpec((tm, LANES), lambda i: (i, 0)),
        compiler_params=_cparams(("parallel",)),
        name="router",
    )(h, wp, bp)


def _moe(h, x, gate, comb, w_gate, w_up, w_down, layer, rpg):
    d_ff = w_gate.shape[-1]
    D = x.shape[1]
    acc = None
    for e in range(N_EXPERTS):
        act = _mm(h, w_gate, (layer, e), d_ff, out_dtype=BF16, w2=w_up, name="moe_up")
        last = e == N_EXPERTS - 1
        acc = _mm(act, w_down, (layer, e), D, out_dtype=F32, rowscale=comb, rowscale_col=e, res=acc,
                  gate=gate if last else None, res2=x if last else None, rows_per_group=rpg,
                  name="moe_down")
    return acc


def _trunk(x, mod, rpg, Bn, T, is_prompt, state_gdn, state_conv_t, caches, invf, p):
    D = x.shape[1]
    d_ff = p["w_gate_d"].shape[-1]
    new_s, new_conv = [], []
    new_kv = [[] for _ in range(N_GROUPS)]
    for i in range(DEPTH):
        j = i // 2
        sh_m, sc_m, gt_m, sh_f, sc_f, gt_f = [mod[i][:, n * D:(n + 1) * D] for n in range(6)]
        h = _norm_mod(x, p["norm_mix"][i], sh_m, sc_m, rpg, BF16 if is_prompt else F32)
        if i % 2 == 0:
            proj = _mm(h, p["w_in_a"], (j,), GDN_QKV_DIM + GDN_VAL_DIM, out_dtype=F32, name="gdn_in")
            ab = _mm_small(h, p["w_in_a"][j][:, GDN_QKV_DIM + GDN_VAL_DIM:], "gdn_in_ab")
            if is_prompt:
                prep = _gdn_prep(proj, ab, p["conv_w_a"], p["a_log"], p["dt_bias"], j, Bn, T)
                og, S = _gdn_scan(prep, proj, p["norm_o_a"], j, Bn, T)
                tail = proj.reshape(Bn, T, -1)[:, T - (CONV_W - 1):, :GDN_QKV_DIM]
                new_conv.append(tail)
            else:
                og, cb_t, S = _gdn_step(proj, ab, state_conv_t, p["conv_w_a"], p["a_log"], p["dt_bias"],
                                        p["norm_o_a"], state_gdn, j)
                new_conv.append(jnp.transpose(cb_t, (1, 0, 2)))
            new_s.append(S)
            x = _mm(og, p["w_out_a"], (j,), D, out_dtype=F32, gate=gt_m, res2=x, rows_per_group=rpg,
                    name="gdn_out")
        else:
            proj = _mm(h, p["w_in_b"], (j,), 3 * N_GROUPS * ATT_DIM, out_dtype=F32, name="att_in")
            if is_prompt:
                projr = _rope(proj, invf, Bn, T, 0, T, 0, 3 * N_GROUPS, BF16)
                os, lses = [], []
                for gi, (window, dil) in enumerate(DIL_GROUPS):
                    o_g, lse_g = _attn_prompt(projr, gi, Bn, T)
                    os.append(o_g)
                    lses.append(lse_g)
                    wn = min(window, T)
                    k_tail = _rope(proj, invf, Bn, T, T - wn, wn, 3 * gi + 1, 1, F32)
                    v_tail = proj.reshape(Bn, T, -1)[:, T - wn:, (3 * gi + 2) * ATT_DIM:(3 * gi + 3) * ATT_DIM]
                    new_kv[gi].append(jnp.stack(
                        [k_tail.reshape(Bn, wn, ATT_H, ATT_DH), v_tail.reshape(Bn, wn, ATT_H, ATT_DH)], axis=2))
                om = _merge_groups(os, lses, BF16)
            else:
                Bs = x.shape[0]
                pr = _rope(proj, invf, 1, Bs, 0, Bs, 0, 3 * N_GROUPS, F32, pos0=PAST_LEN, pos_step=0)
                os, lses = [], []
                for gi in range(N_GROUPS):
                    q3, k3, v3 = [pr[:, (3 * gi + n) * ATT_DIM:(3 * gi + n + 1) * ATT_DIM]
                                  .reshape(Bs, ATT_H, ATT_DH) for n in range(3)]
                    o_g, lse_g = _attn_step(q3, k3, v3, caches[gi], j, gi)
                    os.append(o_g)
                    lses.append(lse_g)
                    new_kv[gi].append(jnp.stack([k3, v3], axis=1).reshape(Bs, 1, 2, ATT_H, ATT_DH))
                om = _merge_step(os, lses).reshape(Bs, ATT_DIM)
            x = _mm(om, p["w_out_b"], (j,), D, out_dtype=F32, gate=gt_m, res2=x, rows_per_group=rpg,
                    name="att_out")
        h = _norm_mod(x, p["norm_ffn"][i], sh_f, sc_f, rpg, BF16 if is_prompt else F32)
        if i % 2 == 0:
            act = _mm(h, p["w_gate_d"], (j,), d_ff, out_dtype=BF16, w2=p["w_up_d"], name="ffn_up")
            x = _mm(act, p["w_down_d"], (j,), D, out_dtype=F32, gate=gt_f, res2=x, rows_per_group=rpg,
                    name="ffn_down")
        else:
            comb = _router(h, p["w_router"][j], p["b_router"][j])
            x = _moe(h, x, gt_f, comb, p["w_gate_e"], p["w_up_e"], p["w_down_e"], j, rpg)
    y = _norm_mod(x, p["norm_final"], None, None, rpg, F32)
    return (y, jnp.stack(new_s), jnp.stack(new_conv),
            jnp.stack(new_kv[0]), jnp.stack(new_kv[1]), jnp.stack(new_kv[2]))


def kernel(x_prompt, x_sample, state_gdn, state_conv, cache_kv_w128, cache_kv_w512, cache_kv_w2048,
           c_prompt, c_sample, w_ada, b_ada, norm_mix, norm_ffn, norm_final,
           w_in_a, conv_w_a, a_log, dt_bias, norm_o_a, w_out_a, w_in_b, w_out_b,
           w_gate_d, w_up_d, w_down_d, w_router, b_router, w_gate_e, w_up_e, w_down_e):
    Bp, T, D = x_prompt.shape
    Bs, Ts, _ = x_sample.shape
    assert Ts == 1 and cache_kv_w128.shape[2] == DIL_GROUPS[0][0]
    p = dict(norm_mix=norm_mix, norm_ffn=norm_ffn, norm_final=norm_final, w_in_a=w_in_a, conv_w_a=conv_w_a,
             a_log=a_log, dt_bias=dt_bias, norm_o_a=norm_o_a, w_out_a=w_out_a, w_in_b=w_in_b, w_out_b=w_out_b,
             w_gate_d=w_gate_d, w_up_d=w_up_d, w_down_d=w_down_d, w_router=w_router, b_router=b_router,
             w_gate_e=w_gate_e, w_up_e=w_up_e, w_down_e=w_down_e)

    n_c = Bp + Bs
    pad = (-n_c) % 8
    c_all = jnp.concatenate([c_prompt, c_sample, jnp.zeros((pad, D), F32)], axis=0)
    mods = [_mm(c_all, w_ada, (l,), 6 * D, out_dtype=F32, pre_silu=True, res=b_ada[l].reshape(1, 6 * D),
                rows_per_group=n_c + pad, name="adaln") for l in range(DEPTH)]
    mod_p = jnp.stack([m[:Bp] for m in mods])
    mod_s = jnp.stack([m[Bp:n_c] for m in mods])

    half = ATT_DH // 2
    inv_freq = ROPE_THETA ** (-jnp.arange(half, dtype=F32) / half)
    invf = jnp.tile(inv_freq, LANES // half).reshape(1, LANES)

    y_p, s_p, cv_p, kv0_p, kv1_p, kv2_p = _trunk(
        x_prompt.reshape(Bp * T, D), mod_p, T, Bp, T, True, None, None, None, invf, p)
    y_s, s_s, cv_s, kv0_s, kv1_s, kv2_s = _trunk(
        x_sample.reshape(Bs, D), mod_s, 1, Bs, 1, False, state_gdn,
        jnp.transpose(state_conv, (0, 2, 1, 3)), (cache_kv_w128, cache_kv_w512, cache_kv_w2048), invf, p)
    return (y_p.reshape(Bp, T, D), y_s.reshape(Bs, Ts, D), s_p, cv_p, kv0_p, kv1_p, kv2_p,
            s_s, cv_s, kv0_s, kv1_s, kv2_s)
```

```python
import functools
import math

import jax
import jax.numpy as jnp
from jax import lax
from jax.experimental import pallas as pl
from jax.experimental.pallas import tpu as pltpu

F32 = jnp.float32
BF16 = jnp.bfloat16

DEPTH = 4
GDN_HK = 8
GDN_HV = 16
GDN_DK = 128
GDN_DV = 128
GDN_KEY_DIM = GDN_HK * GDN_DK
GDN_VAL_DIM = GDN_HV * GDN_DV
GDN_QKV_DIM = 2 * GDN_KEY_DIM + GDN_VAL_DIM
CONV_W = 4
GDN_CHUNK = 64
ATT_H = 16
ATT_DH = 64
ATT_DIM = ATT_H * ATT_DH
DIL_GROUPS = ((128, 1), (512, 4), (2048, 16))
N_GROUPS = len(DIL_GROUPS)
ATT_BLOCK = 128
ROPE_THETA = 10000.0
N_EXPERTS = 8
EPS = 1e-6
NEG_INF = -1e30
PAST_LEN = 2048

V7X_VMEM_BYTES = 64 * 1024 * 1024
VMEM_LIMIT_BYTES = 52 * 1024 * 1024
VMEM_BLOCK_BUDGET = 30 * 1024 * 1024
LANES = 128


def _cparams(sem):
    return pltpu.CompilerParams(dimension_semantics=sem, vmem_limit_bytes=VMEM_LIMIT_BYTES)


def _dot(a, b):
    return jnp.dot(a.astype(BF16), b.astype(BF16), preferred_element_type=F32)


def _dot_nt(a, b):
    return lax.dot_general(a.astype(BF16), b.astype(BF16), (((1,), (1,)), ((), ())),
                           preferred_element_type=F32)


def _dot_tn(a, b):
    return lax.dot_general(a.astype(BF16), b.astype(BF16), (((0,), (0,)), ((), ())),
                           preferred_element_type=F32)


def _split2(a):
    hi = a.astype(BF16)
    lo = (a - hi.astype(F32)).astype(BF16)
    return hi, lo


def _split3(a):
    hi = a.astype(BF16)
    r = a - hi.astype(F32)
    mid = r.astype(BF16)
    lo = (r - mid.astype(F32)).astype(BF16)
    return hi, mid, lo


def _dot3(a, b):
    ah, al = _split2(a)
    bh, bl = _split2(b)
    return _dot(ah, bh) + (_dot(ah, bl) + _dot(al, bh))


def _dot_exact_lhs(c, b, passes=3):
    parts = _split3(b) if passes == 3 else _split2(b)
    acc = _dot(c, parts[0])
    for p in parts[1:]:
        acc = acc + _dot(c, p)
    return acc


def _dot_exact_rhs(a, c, passes=3):
    parts = _split3(a) if passes == 3 else _split2(a)
    acc = _dot(parts[0], c)
    for p in parts[1:]:
        acc = acc + _dot(p, c)
    return acc


def _silu(x):
    return x * jax.nn.sigmoid(x)


def _iota(shape, dim):
    return lax.broadcasted_iota(jnp.int32, shape, dim)


def _norm_mod_body(x_ref, nw_ref, *rest, modulate):
    if modulate:
        sh_ref, sc_ref, o_ref = rest
    else:
        (o_ref,) = rest
    x = x_ref[...].astype(F32)
    y = x * lax.rsqrt(jnp.mean(x * x, axis=-1, keepdims=True) + EPS) * nw_ref[...]
    if modulate:
        y = y * (1.0 + sc_ref[...]) + sh_ref[...]
    o_ref[...] = y.astype(o_ref.dtype)


def _norm_mod(x, nw, shift, scale, rows_per_group, out_dtype):
    M, D = x.shape
    tm = min(M, 1024)
    assert M % tm == 0
    modulate = shift is not None
    in_specs = [pl.BlockSpec((tm, D), lambda i: (i, 0)), pl.BlockSpec((1, D), lambda i: (0, 0))]
    args = [x, nw.reshape(1, D)]
    if modulate:
        if rows_per_group == 1:
            spec = pl.BlockSpec((tm, D), lambda i: (i, 0))
            args += [shift, scale]
        else:
            assert rows_per_group % tm == 0
            spec = pl.BlockSpec((None, 1, D), lambda i: ((i * tm) // rows_per_group, 0, 0))
            args += [shift.reshape(-1, 1, D), scale.reshape(-1, 1, D)]
        in_specs += [spec, spec]
    return pl.pallas_call(
        functools.partial(_norm_mod_body, modulate=modulate),
        out_shape=jax.ShapeDtypeStruct((M, D), out_dtype),
        grid=(M // tm,),
        in_specs=in_specs,
        out_specs=pl.BlockSpec((tm, D), lambda i: (i, 0)),
        compiler_params=_cparams(("parallel",)),
        name="norm_mod",
    )(*args)


def _mm_body(*refs, pre_silu, swiglu, has_rowscale, rowscale_col, has_res, has_gate, has_res2):
    it = iter(refs)
    x_ref = next(it)
    w_ref = next(it)
    w2_ref = next(it) if swiglu else None
    rs_ref = next(it) if has_rowscale else None
    res_ref = next(it) if has_res else None
    gate_ref = next(it) if has_gate else None
    res2_ref = next(it) if has_res2 else None
    o_ref = next(it)
    x = x_ref[...]
    if pre_silu:
        x = _silu(x.astype(F32))
    xb = x.astype(BF16)
    acc = jnp.dot(xb, w_ref[...].astype(BF16), preferred_element_type=F32)
    if swiglu:
        acc = _silu(acc) * jnp.dot(xb, w2_ref[...].astype(BF16), preferred_element_type=F32)
    if has_rowscale:
        acc = acc * rs_ref[...][:, rowscale_col:rowscale_col + 1]
    if has_res:
        acc = acc + res_ref[...]
    if has_gate:
        acc = acc * gate_ref[...]
    if has_res2:
        acc = acc + res2_ref[...]
    o_ref[...] = acc.astype(o_ref.dtype)


def _pick_tiles(M, K, N, x_bytes, n_w, out_bytes, n_res):
    tm = min(M, 1024)
    while M % tm:
        tm //= 2
    cands = [t for t in (1024, 896, 768, 512, 384, 256, 128) if N % t == 0]
    if not cands:
        cands = [N]

    def est(tm_, tn_):
        blocks = tm_ * K * x_bytes + n_w * K * tn_ * 4 + tm_ * tn_ * (out_bytes + 4 * n_res)
        temps = n_w * K * tn_ * 2 + tm_ * K * 2 + (1 + n_w) * tm_ * tn_ * 4
        return 2 * blocks + temps

    ci = 0
    while est(tm, cands[ci]) > VMEM_BLOCK_BUDGET:
        if ci + 1 < len(cands) and cands[ci] > 256:
            ci += 1
        elif tm > 256 and M % (tm // 2) == 0:
            tm //= 2
        elif ci + 1 < len(cands):
            ci += 1
        else:
            break
    return tm, cands[ci]


def _mm(x, w, lead, n_cols, col0=0, *, out_dtype, w2=None, pre_silu=False, rowscale=None,
        rowscale_col=0, res=None, gate=None, res2=None, rows_per_group=1, name="mm"):
    M, K = x.shape
    swiglu = w2 is not None
    n_res = int(res is not None) + int(res2 is not None) + int(gate is not None)
    tm, tn = _pick_tiles(M, K, n_cols, x.dtype.itemsize, 2 if swiglu else 1,
                         jnp.dtype(out_dtype).itemsize, n_res)
    assert col0 % tn == 0 and n_cols % tn == 0
    nj, ni = n_cols // tn, M // tm
    jb0 = col0 // tn
    nlead = len(lead)
    wblock = (None,) * nlead + (K, tn)

    def wmap(j, i):
        return tuple(lead) + (0, jb0 + j)

    in_specs = [pl.BlockSpec((tm, K), lambda j, i: (i, 0)), pl.BlockSpec(wblock, wmap)]
    args = [x, w]
    if swiglu:
        in_specs.append(pl.BlockSpec(wblock, wmap))
        args.append(w2)
    if rowscale is not None:
        in_specs.append(pl.BlockSpec((tm, rowscale.shape[1]), lambda j, i: (i, 0)))
        args.append(rowscale)

    def tile_spec(a):
        if a.shape[0] == M:
            return pl.BlockSpec((tm, tn), lambda j, i: (i, j)), a
        assert rows_per_group % tm == 0 and a.shape[0] * rows_per_group == M
        return (pl.BlockSpec((None, 1, tn), lambda j, i: ((i * tm) // rows_per_group, 0, j)),
                a.reshape(a.shape[0], 1, a.shape[1]))

    for a in (res, gate, res2):
        if a is not None:
            s, a2 = tile_spec(a)
            in_specs.append(s)
            args.append(a2)
    body = functools.partial(
        _mm_body, pre_silu=pre_silu, swiglu=swiglu, has_rowscale=rowscale is not None,
        rowscale_col=rowscale_col, has_res=res is not None, has_gate=gate is not None,
        has_res2=res2 is not None)
    return pl.pallas_call(
        body,
        out_shape=jax.ShapeDtypeStruct((M, n_cols), out_dtype),
        grid=(nj, ni),
        in_specs=in_specs,
        out_specs=pl.BlockSpec((tm, tn), lambda j, i: (i, j)),
        compiler_params=_cparams(("parallel", "parallel")),
        name=name,
    )(*args)


def _mm_small(x, w_small, name):
    M, K = x.shape
    n = w_small.shape[1]
    tm = min(M, 1024)

    def body(x_ref, w_ref, o_ref):
        o_ref[...] = _dot(x_ref[...], w_ref[...])

    return pl.pallas_call(
        body,
        out_shape=jax.ShapeDtypeStruct((M, n), F32),
        grid=(M // tm,),
        in_specs=[pl.BlockSpec((tm, K), lambda i: (i, 0)), pl.BlockSpec((K, n), lambda i: (0, 0))],
        out_specs=pl.BlockSpec((tm, n), lambda i: (i, 0)),
        compiler_params=_cparams(("parallel",)),
        name=name,
    )(x, w_small)


GDN_PREP_ROWS = 512
GDN_PREP_CHUNKS = GDN_PREP_ROWS // GDN_CHUNK
PAIR = 2 * GDN_CHUNK


def _gdn_gates(ab, alog, dtb):
    a = ab[:, :GDN_HV]
    b = ab[:, GDN_HV:]
    xg = a + dtb
    softplus = jnp.maximum(xg, 0.0) + jnp.log1p(jnp.exp(-jnp.abs(xg)))
    return -jnp.exp(alog) * softplus, jax.nn.sigmoid(b)


def _gdn_prep_body(q_ref, qp_ref, k_ref, kp_ref, v_ref, vp_ref, cwq_ref, cwk_ref, cwv_ref,
                   ab_ref, alog_ref, dtb_ref,
                   u_ref, w_ref, qd_ref, kd_ref, qk_ref, gl_ref):
    i = pl.program_id(1)
    hk = pl.program_id(2)
    rows = GDN_PREP_ROWS
    first = i == 0

    def conv(x_ref, xp_ref, cw_ref):
        cur = x_ref[...]
        prev = jnp.where(first, 0.0, xp_ref[...])
        ext = jnp.concatenate([prev, cur], axis=0)
        cw = cw_ref[...]
        y = cur * cw[CONV_W - 1:CONV_W, :]
        for s in range(1, CONV_W):
            y = y + ext[8 - s:8 - s + rows, :] * cw[CONV_W - 1 - s:CONV_W - s, :]
        return _silu(y)

    qc = conv(q_ref, qp_ref, cwq_ref)
    kc = conv(k_ref, kp_ref, cwk_ref)
    vc = conv(v_ref, vp_ref, cwv_ref)
    qn = qc * lax.rsqrt(jnp.sum(qc * qc, axis=-1, keepdims=True) + EPS) * (GDN_DK ** -0.5)
    kn = kc * lax.rsqrt(jnp.sum(kc * kc, axis=-1, keepdims=True) + EPS)

    g16, be16 = _gdn_gates(ab_ref[...], alog_ref[...], dtb_ref[...])
    r16 = _iota((GDN_HV, LANES), 0)

    def head_bcast(x16, hv):
        return _dot_exact_rhs(x16, (r16 == hv).astype(BF16))

    g_b = [head_bcast(g16, 2 * hk + s) for s in (0, 1)]
    be_b = [head_bcast(be16, 2 * hk + s) for s in (0, 1)]

    ri = _iota((PAIR, PAIR), 0)
    ci = _iota((PAIR, PAIR), 1)
    same = (ri >> 6) == (ci >> 6)
    causal = same & (ri >= ci)
    strict = same & (ri > ci)
    blk16 = (ri >> 4) == (ci >> 4)
    tri = causal.astype(BF16)
    eye = (ri == ci).astype(F32)
    r8 = _iota((8, LANES), 0)

    chunks = range(GDN_PREP_CHUNKS)
    sls = [slice(c * GDN_CHUNK, (c + 1) * GDN_CHUNK) for c in chunks]
    k2 = [jnp.concatenate([kn[sl], kn[sl]], axis=0) for sl in sls]
    q2 = [jnp.concatenate([qn[sl], qn[sl]], axis=0) for sl in sls]
    g2 = [jnp.concatenate([g_b[0][sl], g_b[1][sl]], axis=0) for sl in sls]
    b2 = [jnp.concatenate([be_b[0][sl], be_b[1][sl]], axis=0) for sl in sls]

    res = [_dot_exact_lhs(tri, jnp.concatenate([jnp.where(strict, g, 0.0), g], axis=1)) for g in g2]
    kk = [_dot_nt(k, k) for k in k2]
    qkr = [_dot_nt(q, k) for q, k in zip(q2, k2)]
    gcum = [r[:, PAIR:] for r in res]
    decay = [jnp.where(causal, jnp.exp(jnp.where(causal, r[:, :PAIR], 0.0)), 0.0) for r in res]
    eg = [jnp.exp(g) for g in gcum]
    lmat = [jnp.where(strict, b * k * d, 0.0) for b, k, d in zip(b2, kk, decay)]

    ld = [jnp.where(blk16, l, 0.0) for l in lmat]
    nn = [jnp.where(blk16, 0.0, l) for l in lmat]
    x = [eye - l for l in ld]
    p = [_dot(l, l) for l in ld]
    for _ in range(2):
        x = [xi + _dot(xi, pi) for xi, pi in zip(x, p)]
        p = [_dot(pi, pi) for pi in p]
    x = [xi + _dot(xi, pi) for xi, pi in zip(x, p)]
    v2 = [jnp.concatenate([vc[sl, :GDN_DV], vc[sl, GDN_DV:]], axis=0) for sl in sls]
    rhs = [jnp.concatenate([b * v, b * e * k], axis=1) for b, v, e, k in zip(b2, v2, eg, k2)]
    mm = [_dot(xi, n) for xi, n in zip(x, nn)]
    y = [_dot(xi, r) for xi, r in zip(x, rhs)]
    m2 = [_dot(m, m) for m in mm]
    y = [yi + _dot(m, yi) for yi, m in zip(y, m2)]
    y = [yi - _dot(m, yi) for yi, m in zip(y, mm)]

    for c in chunks:
        gl0 = gcum[c][GDN_CHUNK - 1:GDN_CHUNK, :]
        gl1 = gcum[c][PAIR - 1:PAIR, :]
        glast = jnp.where(ri < GDN_CHUNK, gl0, gl1)
        u_ref[c] = y[c][:, :GDN_DV]
        w_ref[c] = y[c][:, GDN_DV:].astype(BF16)
        qd_ref[c] = (q2[c] * eg[c]).astype(BF16)
        kd_ref[c] = (k2[c] * jnp.exp(glast - gcum[c])).astype(BF16)
        qk_ref[c] = (qkr[c] * decay[c]).astype(BF16)
        gl_ref[c] = jnp.where(r8 == 0, jnp.exp(gl0), jnp.where(r8 == 1, jnp.exp(gl1), 0.0))


def _gdn_prep(proj, ab, conv_w, a_log, dt_bias, layer, Bn, T):
    rows = GDN_PREP_ROWS
    nblk = T // rows
    nch = T // GDN_CHUNK
    cpb = GDN_PREP_CHUNKS

    def cur(width, col):
        return pl.BlockSpec((rows, width), lambda b, i, h: (b * nblk + i, col(h)))

    def prev(width, col):
        return pl.BlockSpec((8, width),
                            lambda b, i, h: (jnp.maximum((b * T + i * rows) // 8 - 1, 0), col(h)))

    def cw(width, col):
        return pl.BlockSpec((None, CONV_W, width), lambda b, i, h: (layer, 0, col(h)))

    qcol = lambda h: h
    kcol = lambda h: GDN_HK + h
    vcol = lambda h: GDN_HK + h
    in_specs = [cur(128, qcol), prev(128, qcol), cur(128, kcol), prev(128, kcol),
                cur(256, vcol), prev(256, vcol),
                cw(128, qcol), cw(128, kcol), cw(256, vcol),
                pl.BlockSpec((rows, 2 * GDN_HV), lambda b, i, h: (b * nblk + i, 0)),
                pl.BlockSpec((None, 1, GDN_HV), lambda b, i, h: (layer, 0, 0)),
                pl.BlockSpec((None, 1, GDN_HV), lambda b, i, h: (layer, 0, 0))]
    big = lambda dt: jax.ShapeDtypeStruct((Bn, GDN_HK, nch, PAIR, LANES), dt)
    ospec = pl.BlockSpec((None, None, cpb, PAIR, LANES), lambda b, i, h: (b, h, i, 0, 0))
    out_shape = (big(F32), big(BF16), big(BF16), big(BF16), big(BF16),
                 jax.ShapeDtypeStruct((Bn, GDN_HK, nch, 8, LANES), F32))
    out_specs = (ospec, ospec, ospec, ospec, ospec,
                 pl.BlockSpec((None, None, cpb, 8, LANES), lambda b, i, h: (b, h, i, 0, 0)))
    return pl.pallas_call(
        _gdn_prep_body,
        out_shape=out_shape,
        grid=(Bn, nblk, GDN_HK),
        in_specs=in_specs,
        out_specs=out_specs,
        compiler_params=_cparams(("parallel", "parallel", "parallel")),
        name="gdn_prep",
    )(proj, proj, proj, proj, proj, proj, conv_w, conv_w, conv_w, ab,
      a_log.reshape(-1, 1, GDN_HV), dt_bias.reshape(-1, 1, GDN_HV))


GDN_SCAN_ROWS = 256
GDN_SCAN_CHUNKS = GDN_SCAN_ROWS // GDN_CHUNK


def _gated_norm_store(o_scr, z_ref, no_ref, og_ref):
    for h in range(GDN_HV):
        sl = slice(h * GDN_DV, (h + 1) * GDN_DV)
        oh = o_scr[:, sl]
        zh = z_ref[:, sl]
        on = oh * lax.rsqrt(jnp.mean(oh * oh, axis=-1, keepdims=True) + EPS) * no_ref[...]
        og_ref[:, sl] = (on * _silu(zh)).astype(og_ref.dtype)


def _gdn_scan_body(u_ref, w_ref, qd_ref, kd_ref, qk_ref, gl_ref, z_ref, no_ref,
                   og_ref, s_ref, o_scr):
    @pl.when(pl.program_id(1) == 0)
    def _():
        s_ref[...] = jnp.zeros_like(s_ref)

    C = GDN_CHUNK
    heads = range(GDN_HV)
    for c in range(GDN_SCAN_CHUNKS):
        half = lambda ref, hv: ref[hv // 2, c][(hv % 2) * C:(hv % 2 + 1) * C]
        s = [s_ref[hv] for hv in heads]
        ws = [_dot(half(w_ref, hv), s[hv]) for hv in heads]
        qs = [_dot(half(qd_ref, hv), s[hv]) for hv in heads]
        vn = [half(u_ref, hv) - ws[hv] for hv in heads]
        oi = [_dot(qk_ref[hk, c], jnp.concatenate([vn[2 * hk], vn[2 * hk + 1]], axis=0))
              for hk in range(GDN_HK)]
        kv = [_dot_tn(half(kd_ref, hv), vn[hv]) for hv in heads]
        for hv in heads:
            gl = gl_ref[hv // 2, c]
            s_ref[hv] = s[hv] * gl[hv % 2:hv % 2 + 1, :] + kv[hv]
            o_scr[c * C:(c + 1) * C, hv * GDN_DV:(hv + 1) * GDN_DV] = (
                qs[hv] + oi[hv // 2][(hv % 2) * C:(hv % 2 + 1) * C])
    _gated_norm_store(o_scr, z_ref, no_ref, og_ref)


def _gdn_scan(prep, proj, norm_o, layer, Bn, T):
    u, w, qd, kd, qk, gl = prep
    rows = GDN_SCAN_ROWS
    nblk = T // rows
    cpb = GDN_SCAN_CHUNKS
    bspec = pl.BlockSpec((None, GDN_HK, cpb, PAIR, LANES), lambda b, i: (b, 0, i, 0, 0))
    in_specs = [bspec, bspec, bspec, bspec, bspec,
                pl.BlockSpec((None, GDN_HK, cpb, 8, LANES), lambda b, i: (b, 0, i, 0, 0)),
                pl.BlockSpec((rows, GDN_VAL_DIM), lambda b, i: (b * nblk + i, GDN_QKV_DIM // GDN_VAL_DIM)),
                pl.BlockSpec((None, 1, GDN_DV), lambda b, i: (layer, 0, 0))]
    out_shape = (jax.ShapeDtypeStruct((Bn * T, GDN_VAL_DIM), BF16),
                 jax.ShapeDtypeStruct((Bn, GDN_HV, GDN_DK, GDN_DV), F32))
    out_specs = (pl.BlockSpec((rows, GDN_VAL_DIM), lambda b, i: (b * nblk + i, 0)),
                 pl.BlockSpec((None, GDN_HV, GDN_DK, GDN_DV), lambda b, i: (b, 0, 0, 0)))
    return pl.pallas_call(
        _gdn_scan_body,
        out_shape=out_shape,
        grid=(Bn, nblk),
        in_specs=in_specs,
        out_specs=out_specs,
        scratch_shapes=[pltpu.VMEM((rows, GDN_VAL_DIM), F32)],
        compiler_params=_cparams(("parallel", "arbitrary")),
        name="gdn_scan",
    )(u, w, qd, kd, qk, gl, proj, norm_o.reshape(-1, 1, GDN_DV))


GDN_STEP_ROWS = 8


def _gdn_step_body(proj_ref, ab_ref, cb_ref, cw_ref, alog_ref, dtb_ref, no_ref, st_ref,
                   og_ref, cbo_ref, sto_ref, o_scr):
    R = GDN_STEP_ROWS
    qkv = proj_ref[:, :GDN_QKV_DIM]
    cw = cw_ref[...]
    y = cb_ref[0] * cw[0:1, :]
    y = y + cb_ref[1] * cw[1:2, :]
    y = y + cb_ref[2] * cw[2:3, :]
    y = y + qkv * cw[3:4, :]
    qkv_c = _silu(y)
    cbo_ref[0] = cb_ref[1]
    cbo_ref[1] = cb_ref[2]
    cbo_ref[2] = qkv

    g16, be16 = _gdn_gates(ab_ref[...], alog_ref[...], dtb_ref[...])
    expand = ((_iota((GDN_HV, GDN_VAL_DIM), 1) >> 7) == _iota((GDN_HV, GDN_VAL_DIM), 0)).astype(BF16)
    eg_f = jnp.exp(_dot_exact_rhs(g16, expand))
    be_f = _dot_exact_rhs(be16, expand)
    r8 = _iota((8, LANES), 0)

    for hk in range(GDN_HK):
        qh = qkv_c[:, hk * GDN_DK:(hk + 1) * GDN_DK]
        kh = qkv_c[:, GDN_KEY_DIM + hk * GDN_DK:GDN_KEY_DIM + (hk + 1) * GDN_DK]
        qn = qh * lax.rsqrt(jnp.sum(qh * qh, axis=-1, keepdims=True) + EPS) * (GDN_DK ** -0.5)
        kn = kh * lax.rsqrt(jnp.sum(kh * kh, axis=-1, keepdims=True) + EPS)
        qk = jnp.sum(qn * kn, axis=-1, keepdims=True)
        for s in (0, 1):
            hv = 2 * hk + s
            sl = slice(hv * GDN_DV, (hv + 1) * GDN_DV)
            eg = eg_f[:, sl]
            be = be_f[:, sl]
            vh = qkv_c[:, 2 * GDN_KEY_DIM + hv * GDN_DV:2 * GDN_KEY_DIM + (hv + 1) * GDN_DV]
            w_rows = be * eg * kn
            qd_rows = qn * eg
            rows = range(R)
            st = [st_ref[bb, hv] for bb in rows]
            rr = [_dot(jnp.where(r8 == 0, w_rows[bb:bb + 1], jnp.where(r8 == 1, qd_rows[bb:bb + 1], 0.0)),
                       st[bb]) for bb in rows]
            vn = [be[bb:bb + 1] * vh[bb:bb + 1] - rr[bb][0:1] for bb in rows]
            kv = [_dot_tn(jnp.where(r8 == 0, kn[bb:bb + 1], 0.0), jnp.where(r8 == 0, vn[bb], 0.0))
                  for bb in rows]
            for bb in rows:
                sto_ref[bb, hv] = st[bb] * eg[bb:bb + 1] + kv[bb]
                o_scr[bb:bb + 1, sl] = rr[bb][1:2] + qk[bb:bb + 1] * vn[bb]
    _gated_norm_store(o_scr, proj_ref.at[:, GDN_QKV_DIM:], no_ref, og_ref)


def _gdn_step(proj, ab, conv_buf_t, conv_w, a_log, dt_bias, norm_o, state, layer):
    Bs = proj.shape[0]
    R = GDN_STEP_ROWS
    vec = lambda n: pl.BlockSpec((None, 1, n), lambda i: (layer, 0, 0))
    in_specs = [pl.BlockSpec((R, proj.shape[1]), lambda i: (i, 0)),
                pl.BlockSpec((R, 2 * GDN_HV), lambda i: (i, 0)),
                pl.BlockSpec((None, CONV_W - 1, R, GDN_QKV_DIM), lambda i: (layer, 0, i, 0)),
                pl.BlockSpec((None, CONV_W, GDN_QKV_DIM), lambda i: (layer, 0, 0)),
                vec(GDN_HV), vec(GDN_HV), vec(GDN_DV),
                pl.BlockSpec((None, R, GDN_HV, GDN_DK, GDN_DV), lambda i: (layer, i, 0, 0, 0))]
    out_shape = (jax.ShapeDtypeStruct((Bs, GDN_VAL_DIM), F32),
                 jax.ShapeDtypeStruct((CONV_W - 1, Bs, GDN_QKV_DIM), F32),
                 jax.ShapeDtypeStruct((Bs, GDN_HV, GDN_DK, GDN_DV), F32))
    out_specs = (pl.BlockSpec((R, GDN_VAL_DIM), lambda i: (i, 0)),
                 pl.BlockSpec((CONV_W - 1, R, GDN_QKV_DIM), lambda i: (0, i, 0)),
                 pl.BlockSpec((R, GDN_HV, GDN_DK, GDN_DV), lambda i: (i, 0, 0, 0)))
    return pl.pallas_call(
        _gdn_step_body,
        out_shape=out_shape,
        grid=(Bs // R,),
        in_specs=in_specs,
        out_specs=out_specs,
        scratch_shapes=[pltpu.VMEM((R, GDN_VAL_DIM), F32)],
        compiler_params=_cparams(("parallel",)),
        name="gdn_step",
    )(proj, ab, conv_buf_t, conv_w, a_log.reshape(-1, 1, GDN_HV), dt_bias.reshape(-1, 1, GDN_HV),
      norm_o.reshape(-1, 1, GDN_DV), state)


def _rope_tables(ang):
    lane = _iota(ang.shape, 1)
    cs = jnp.cos(ang)
    sn = jnp.sin(ang)
    sg = jnp.where((lane & (ATT_DH - 1)) < ATT_DH // 2, -sn, sn)
    reps = ATT_DIM // LANES
    return jnp.concatenate([cs] * reps, axis=1), jnp.concatenate([sg] * reps, axis=1)


def _rope_apply(x, cs, sg):
    lane = _iota(x.shape, 1)
    first_half = (lane & (ATT_DH - 1)) < ATT_DH // 2
    half = ATT_DH // 2
    partner = jnp.where(first_half, pltpu.roll(x, x.shape[1] - half, 1), pltpu.roll(x, half, 1))
    return x * cs + partner * sg


def _rope_body(invf_ref, x_ref, o_ref, *, pos0, pos_step, row0, tr, col0):
    i = pl.program_id(1)
    c = pl.program_id(2)
    is_v = lax.rem(col0 + c, 3) == 2

    @pl.when(is_v)
    def _():
        o_ref[...] = x_ref[...].astype(o_ref.dtype)

    @pl.when(jnp.logical_not(is_v))
    def _():
        pos = (pos0 + pos_step * (row0 + i * tr + _iota((tr, LANES), 0))).astype(F32)
        cs, sg = _rope_tables(pos * invf_ref[...])
        o_ref[...] = _rope_apply(x_ref[...], cs, sg).astype(o_ref.dtype)


def _rope(proj, invf, Bn, T, row0, nrows, col0, ncols, out_dtype, pos0=0, pos_step=1):
    tr = min(nrows, 256)
    assert nrows % tr == 0 and row0 % tr == 0 and T % tr == 0
    nb = nrows // tr
    body = functools.partial(_rope_body, pos0=pos0, pos_step=pos_step, row0=row0, tr=tr, col0=col0)
    return pl.pallas_call(
        body,
        out_shape=jax.ShapeDtypeStruct((Bn * nrows, ncols * ATT_DIM), out_dtype),
        grid=(Bn, nb, ncols),
        in_specs=[pl.BlockSpec((1, LANES), lambda b, i, c: (0, 0)),
                  pl.BlockSpec((tr, ATT_DIM), lambda b, i, c: (b * (T // tr) + row0 // tr + i, col0 + c))],
        out_specs=pl.BlockSpec((tr, ATT_DIM), lambda b, i, c: (b * nb + i, c)),
        compiler_params=_cparams(("parallel", "parallel", "parallel")),
        name="rope",
    )(invf, proj)


def _attn_body(q_ref, kp_ref, kc_ref, vp_ref, vc_ref, o_ref, lse_ref):
    n = pl.program_id(2)
    blk = ATT_BLOCK
    qi = _iota((blk, 2 * blk), 0)
    kj = _iota((blk, 2 * blk), 1)
    dist = qi + blk - kj
    n_win = blk
    mask = (dist >= 0) & (dist <= n_win) & ((kj >= blk) | (n > 0))
    lane = _iota((blk, LANES), 1)
    lo = lane < ATT_DH
    lse_acc = jnp.zeros((blk, LANES), F32)
    scale = ATT_DH ** -0.5
    for hp in range(ATT_H // 2):
        sl = slice(hp * LANES, (hp + 1) * LANES)
        q2 = q_ref[:, sl].astype(F32)
        k2 = jnp.concatenate([kp_ref[:, sl], kc_ref[:, sl]], axis=0)
        v2 = jnp.concatenate([vp_ref[:, sl], vc_ref[:, sl]], axis=0)
        outs = []
        for s in (0, 1):
            qm = jnp.where(lo if s == 0 else jnp.logical_not(lo), q2, 0.0)
            sc = _dot_nt(qm, k2) * scale
            sc = jnp.where(mask, sc, NEG_INF)
            m = jnp.max(sc, axis=-1, keepdims=True)
            p = jnp.exp(sc - m)
            l = jnp.sum(p, axis=-1, keepdims=True)
            outs.append(_dot(p, v2) * (1.0 / l))
            lse_acc = jnp.where(lane == 2 * hp + s, m + jnp.log(l), lse_acc)
        o_ref[:, sl] = jnp.where(lo, outs[0], outs[1])
    lse_ref[...] = lse_acc


def _attn_prompt(qkv, gi, Bn, T):
    dil = DIL_GROUPS[gi][1]
    Td = T // dil
    nb = Td // ATT_BLOCK

    def spec(which, prev):
        def imap(b, r, n):
            return (b, r, jnp.maximum(n - 1, 0) if prev else n, which)
        return pl.BlockSpec((None, None, ATT_BLOCK, ATT_DIM), imap)

    return pl.pallas_call(
        _attn_body,
        out_shape=(jax.ShapeDtypeStruct((Bn, dil, Td, ATT_DIM), F32),
                   jax.ShapeDtypeStruct((Bn, dil, Td, LANES), F32)),
        grid=(Bn, dil, nb),
        in_specs=[spec(0, False), spec(1, True), spec(1, False), spec(2, True), spec(2, False)],
        out_specs=(pl.BlockSpec((None, None, ATT_BLOCK, ATT_DIM), lambda b, r, n: (b, r, n, 0)),
                   pl.BlockSpec((None, None, ATT_BLOCK, LANES), lambda b, r, n: (b, r, n, 0))),
        compiler_params=_cparams(("parallel", "parallel", "parallel")),
        name="attn_prompt",
    )(qkv, qkv, qkv, qkv, qkv)


ROPE_SPLIT_ROWS = 256


def _stride_perm(rows, dil, transpose=False):
    n = rows // dil
    i = _iota((rows, rows), 1 if transpose else 0)
    j = _iota((rows, rows), 0 if transpose else 1)
    shift = n.bit_length() - 1
    assert n == 1 << shift
    return (j == (i & (n - 1)) * dil + (i >> shift)).astype(BF16)


def _rope_split_body(invf_ref, x_ref, o0_ref, o1_ref, o2_ref):
    rows = ROPE_SPLIT_ROWS
    pos = (pl.program_id(1) * rows + _iota((rows, LANES), 0)).astype(F32)
    cs, sg = _rope_tables(pos * invf_ref[...])
    for gi, o_ref in enumerate((o0_ref, o1_ref, o2_ref)):
        dil = DIL_GROUPS[gi][1]
        n = rows // dil
        perm = _stride_perm(rows, dil) if dil > 1 else None
        for which in range(3):
            xx = x_ref[:, (3 * gi + which) * ATT_DIM:(3 * gi + which + 1) * ATT_DIM]
            if which < 2:
                xx = _rope_apply(xx, cs, sg)
            xb = xx.astype(BF16)
            if perm is not None:
                xb = jnp.dot(perm, xb, preferred_element_type=F32).astype(BF16)
            for r in range(dil):
                o_ref[r, :, which * ATT_DIM:(which + 1) * ATT_DIM] = xb[r * n:(r + 1) * n, :]


def _rope_split(proj, invf, Bn, T):
    rows = ROPE_SPLIT_ROWS
    nblk = T // rows
    out_shape, out_specs = [], []
    for _, dil in DIL_GROUPS:
        out_shape.append(jax.ShapeDtypeStruct((Bn, dil, T // dil, 3 * ATT_DIM), BF16))
        out_specs.append(pl.BlockSpec((None, dil, rows // dil, 3 * ATT_DIM), lambda b, i: (b, 0, i, 0)))
    return pl.pallas_call(
        _rope_split_body,
        out_shape=tuple(out_shape),
        grid=(Bn, nblk),
        in_specs=[pl.BlockSpec((1, LANES), lambda b, i: (0, 0)),
                  pl.BlockSpec((rows, 3 * N_GROUPS * ATT_DIM), lambda b, i: (b * nblk + i, 0))],
        out_specs=tuple(out_specs),
        compiler_params=_cparams(("parallel", "parallel")),
        name="rope_split",
    )(invf, proj)


MERGE_ROWS = 256


def _merge_body(o0_ref, o1_ref, o2_ref, l0_ref, l1_ref, l2_ref, out_ref):
    rows = MERGE_ROWS

    def position_order(ref, gi, passes):
        dil = DIL_GROUPS[gi][1]
        x = jnp.concatenate([ref[r] for r in range(dil)], axis=0)
        if dil == 1:
            return x
        return _dot_exact_lhs(_stride_perm(rows, dil, transpose=True), x, passes=passes)

    ls = [position_order(l_ref, gi, 3) for gi, l_ref in enumerate((l0_ref, l1_ref, l2_ref))]
    mx = jnp.maximum(jnp.maximum(ls[0], ls[1]), ls[2])
    es = [jnp.exp(l - mx) for l in ls]
    inv = 1.0 / (es[0] + es[1] + es[2])
    expand = ((_iota((LANES, ATT_DIM), 1) >> 6) == _iota((LANES, ATT_DIM), 0)).astype(BF16)
    acc = None
    for gi, o_ref in enumerate((o0_ref, o1_ref, o2_ref)):
        term = _dot_exact_rhs(es[gi] * inv, expand, passes=2) * position_order(o_ref, gi, 2)
        acc = term if acc is None else acc + term
    out_ref[...] = acc.astype(out_ref.dtype)


def _merge_groups(os, lses, Bn, T, out_dtype):
    rows = MERGE_ROWS
    nblk = T // rows
    ospecs, lspecs = [], []
    for _, dil in DIL_GROUPS:
        ospecs.append(pl.BlockSpec((None, dil, rows // dil, ATT_DIM), lambda b, i: (b, 0, i, 0)))
        lspecs.append(pl.BlockSpec((None, dil, rows // dil, LANES), lambda b, i: (b, 0, i, 0)))
    return pl.pallas_call(
        _merge_body,
        out_shape=jax.ShapeDtypeStruct((Bn * T, ATT_DIM), out_dtype),
        grid=(Bn, nblk),
        in_specs=ospecs + lspecs,
        out_specs=pl.BlockSpec((rows, ATT_DIM), lambda b, i: (b * nblk + i, 0)),
        compiler_params=_cparams(("parallel", "parallel")),
        name="attn_merge",
    )(*os, *lses)


ATT_STEP_CACHE_BLOCK_BYTES = 16 * 1024 * 1024


def _attn_step_body(q_ref, k_ref, v_ref, c_ref, o_ref, lse_ref, *, rows, dil):
    scale = ATT_DH ** -0.5
    L = c_ref.shape[-1]
    valid = (_iota((8, L), 1) & (dil - 1)) == 0
    r8 = _iota((8, ATT_DH), 0)
    r8l = _iota((8, L), 0)
    items = [(bb, hg) for bb in range(rows) for hg in range(ATT_H // 8)]
    hsl = lambda hg: slice(8 * hg, 8 * hg + 8)
    q8s = [q_ref[bb, hsl(hg), :] for bb, hg in items]
    s8s = []
    for (bb, hg), q8 in zip(items, q8s):
        s8 = jnp.zeros((8, L), F32)
        for hh in range(8):
            s8 = jnp.where(r8l == hh, _dot(q8, c_ref[bb, 0, 8 * hg + hh]), s8)
        s8s.append(jnp.where(valid, s8 * scale, NEG_INF))
    ps, pns, ls, ms = [], [], [], []
    for (bb, hg), q8, s8 in zip(items, q8s, s8s):
        sn = jnp.sum(q8 * k_ref[bb, hsl(hg), :], axis=-1, keepdims=True) * scale
        m = jnp.maximum(jnp.max(s8, axis=-1, keepdims=True), sn)
        p = jnp.exp(s8 - m)
        pn = jnp.exp(sn - m)
        ps.append(p)
        pns.append(pn)
        ms.append(m)
        ls.append(jnp.sum(p, axis=-1, keepdims=True) + pn)
    for (bb, hg), p, pn, l, m in zip(items, ps, pns, ls, ms):
        o8 = jnp.zeros((8, ATT_DH), F32)
        for hh in range(8):
            o8 = jnp.where(r8 == hh, _dot_nt(p, c_ref[bb, 1, 8 * hg + hh]), o8)
        o_ref[bb, hsl(hg), :] = (o8 + pn * v_ref[bb, hsl(hg), :]) * (1.0 / l)
        lse_ref[bb, hsl(hg), :] = jnp.broadcast_to(m + jnp.log(l), (8, ATT_DH))


def _attn_step(q3, k3, v3, cache, layer, gi):
    Bs = q3.shape[0]
    window, dil = DIL_GROUPS[gi]
    L = cache.shape[2]
    assert L == window and L % dil == 0 and dil & (dil - 1) == 0
    ct = jnp.transpose(cache, (0, 1, 3, 4, 5, 2))
    R = max(1, min(8, ATT_STEP_CACHE_BLOCK_BYTES // (2 * ATT_DIM * L * 4)))
    assert Bs % R == 0
    rspec = pl.BlockSpec((R, ATT_H, ATT_DH), lambda i: (i, 0, 0))
    return pl.pallas_call(
        functools.partial(_attn_step_body, rows=R, dil=dil),
        out_shape=(jax.ShapeDtypeStruct((Bs, ATT_H, ATT_DH), F32),
                   jax.ShapeDtypeStruct((Bs, ATT_H, ATT_DH), F32)),
        grid=(Bs // R,),
        in_specs=[rspec, rspec, rspec,
                  pl.BlockSpec((None, R, 2, ATT_H, ATT_DH, L), lambda i: (layer, i, 0, 0, 0, 0))],
        out_specs=(rspec, rspec),
        compiler_params=_cparams(("parallel",)),
        name="attn_step",
    )(q3, k3, v3, ct)


def _merge_step_body(o0_ref, o1_ref, o2_ref, l0_ref, l1_ref, l2_ref, out_ref):
    l0, l1, l2 = l0_ref[...], l1_ref[...], l2_ref[...]
    mx = jnp.maximum(jnp.maximum(l0, l1), l2)
    e0, e1, e2 = jnp.exp(l0 - mx), jnp.exp(l1 - mx), jnp.exp(l2 - mx)
    inv = 1.0 / (e0 + e1 + e2)
    out_ref[...] = (e0 * inv) * o0_ref[...] + (e1 * inv) * o1_ref[...] + (e2 * inv) * o2_ref[...]


def _merge_step(os, lses):
    shape = os[0].shape
    spec = pl.BlockSpec(shape, lambda i: (0, 0, 0))
    return pl.pallas_call(
        _merge_step_body,
        out_shape=jax.ShapeDtypeStruct(shape, F32),
        grid=(1,),
        in_specs=[spec] * 6,
        out_specs=spec,
        compiler_params=_cparams(("arbitrary",)),
        name="attn_merge_step",
    )(*os, *lses)


def _router_body(h_ref, w_ref, b_ref, o_ref):
    logits = _dot3(h_ref[...].astype(F32), w_ref[...]) + b_ref[...]
    lane = _iota(logits.shape, 1)
    logits = jnp.where(lane < N_EXPERTS, logits, -jnp.inf)
    m1 = jnp.max(logits, axis=-1, keepdims=True)
    i1 = jnp.min(jnp.where(logits == m1, lane, LANES), axis=-1, keepdims=True)
    rest = jnp.where(lane == i1, -jnp.inf, logits)
    m2 = jnp.max(rest, axis=-1, keepdims=True)
    i2 = jnp.min(jnp.where(rest == m2, lane, LANES), axis=-1, keepdims=True)
    e2 = jnp.exp(m2 - m1)
    g1 = 1.0 / (1.0 + e2)
    g2 = e2 * g1
    o_ref[...] = jnp.where(lane == i1, g1, 0.0) + jnp.where(lane == i2, g2, 0.0)


def _router(h, w_router, b_router):
    M, D = h.shape
    tm = min(M, 1024)
    wp = jnp.pad(w_router, ((0, 0), (0, LANES - N_EXPERTS)))
    bp = jnp.pad(b_router, (0, LANES - N_EXPERTS)).reshape(1, LANES)
    return pl.pallas_call(
        _router_body,
        out_shape=jax.ShapeDtypeStruct((M, LANES), F32),
        grid=(M // tm,),
        in_specs=[pl.BlockSpec((tm, D), lambda i: (i, 0)),
                  pl.BlockSpec((D, LANES), lambda i: (0, 0)),
                  pl.BlockSpec((1, LANES), lambda i: (0, 0))],
        out_specs=pl.BlockSpec((tm, LANES), lambda i: (i, 0)),
        compiler_params=_cparams(("parallel",)),
        name="router",
    )(h, wp, bp)


def _moe(h, x, gate, comb, w_gate, w_up, w_down, layer, rpg):
    d_ff = w_gate.shape[-1]
    D = x.shape[1]
    acc = None
    for e in range(N_EXPERTS):
        act = _mm(h, w_gate, (layer, e), d_ff, out_dtype=BF16, w2=w_up, name="moe_up")
        last = e == N_EXPERTS - 1
        acc = _mm(act, w_down, (layer, e), D, out_dtype=F32, rowscale=comb, rowscale_col=e, res=acc,
                  gate=gate if last else None, res2=x if last else None, rows_per_group=rpg,
                  name="moe_down")
    return acc


def _trunk(x, mod, rpg, Bn, T, is_prompt, state_gdn, state_conv_t, caches, invf, p):
    D = x.shape[1]
    d_ff = p["w_gate_d"].shape[-1]
    new_s, new_conv = [], []
    new_kv = [[] for _ in range(N_GROUPS)]
    for i in range(DEPTH):
        j = i // 2
        sh_m, sc_m, gt_m, sh_f, sc_f, gt_f = [mod[i][:, n * D:(n + 1) * D] for n in range(6)]
        h = _norm_mod(x, p["norm_mix"][i], sh_m, sc_m, rpg, BF16 if is_prompt else F32)
        if i % 2 == 0:
            proj = _mm(h, p["w_in_a"], (j,), GDN_QKV_DIM + GDN_VAL_DIM, out_dtype=F32, name="gdn_in")
            ab = _mm_small(h, p["w_in_a"][j][:, GDN_QKV_DIM + GDN_VAL_DIM:], "gdn_in_ab")
            if is_prompt:
                prep = _gdn_prep(proj, ab, p["conv_w_a"], p["a_log"], p["dt_bias"], j, Bn, T)
                og, S = _gdn_scan(prep, proj, p["norm_o_a"], j, Bn, T)
                tail = proj.reshape(Bn, T, -1)[:, T - (CONV_W - 1):, :GDN_QKV_DIM]
                new_conv.append(tail)
            else:
                og, cb_t, S = _gdn_step(proj, ab, state_conv_t, p["conv_w_a"], p["a_log"], p["dt_bias"],
                                        p["norm_o_a"], state_gdn, j)
                new_conv.append(jnp.transpose(cb_t, (1, 0, 2)))
            new_s.append(S)
            x = _mm(og, p["w_out_a"], (j,), D, out_dtype=F32, gate=gt_m, res2=x, rows_per_group=rpg,
                    name="gdn_out")
        else:
            proj = _mm(h, p["w_in_b"], (j,), 3 * N_GROUPS * ATT_DIM, out_dtype=F32, name="att_in")
            if is_prompt:
                qkvs = _rope_split(proj, invf, Bn, T)
                os, lses = [], []
                for gi, (window, dil) in enumerate(DIL_GROUPS):
                    o_g, lse_g = _attn_prompt(qkvs[gi], gi, Bn, T)
                    os.append(o_g)
                    lses.append(lse_g)
                    wn = min(window, T)
                    k_tail = _rope(proj, invf, Bn, T, T - wn, wn, 3 * gi + 1, 1, F32)
                    v_tail = proj.reshape(Bn, T, -1)[:, T - wn:, (3 * gi + 2) * ATT_DIM:(3 * gi + 3) * ATT_DIM]
                    new_kv[gi].append(jnp.stack(
                        [k_tail.reshape(Bn, wn, ATT_H, ATT_DH), v_tail.reshape(Bn, wn, ATT_H, ATT_DH)], axis=2))
                om = _merge_groups(os, lses, Bn, T, BF16)
            else:
                Bs = x.shape[0]
                pr = _rope(proj, invf, 1, Bs, 0, Bs, 0, 3 * N_GROUPS, F32, pos0=PAST_LEN, pos_step=0)
                os, lses = [], []
                for gi in range(N_GROUPS):
                    q3, k3, v3 = [pr[:, (3 * gi + n) * ATT_DIM:(3 * gi + n + 1) * ATT_DIM]
                                  .reshape(Bs, ATT_H, ATT_DH) for n in range(3)]
                    o_g, lse_g = _attn_step(q3, k3, v3, caches[gi], j, gi)
                    os.append(o_g)
                    lses.append(lse_g)
                    new_kv[gi].append(jnp.stack([k3, v3], axis=1).reshape(Bs, 1, 2, ATT_H, ATT_DH))
                om = _merge_step(os, lses).reshape(Bs, ATT_DIM)
            x = _mm(om, p["w_out_b"], (j,), D, out_dtype=F32, gate=gt_m, res2=x, rows_per_group=rpg,
                    name="att_out")
        h = _norm_mod(x, p["norm_ffn"][i], sh_f, sc_f, rpg, BF16 if is_prompt else F32)
        if i % 2 == 0:
            act = _mm(h, p["w_gate_d"], (j,), d_ff, out_dtype=BF16, w2=p["w_up_d"], name="ffn_up")
            x = _mm(act, p["w_down_d"], (j,), D, out_dtype=F32, gate=gt_f, res2=x, rows_per_group=rpg,
                    name="ffn_down")
        else:
            comb = _router(h, p["w_router"][j], p["b_router"][j])
            x = _moe(h, x, gt_f, comb, p["w_gate_e"], p["w_up_e"], p["w_down_e"], j, rpg)
    y = _norm_mod(x, p["norm_final"], None, None, rpg, F32)
    return (y, jnp.stack(new_s), jnp.stack(new_conv),
            jnp.stack(new_kv[0]), jnp.stack(new_kv[1]), jnp.stack(new_kv[2]))


def kernel(x_prompt, x_sample, state_gdn, state_conv, cache_kv_w128, cache_kv_w512, cache_kv_w2048,
           c_prompt, c_sample, w_ada, b_ada, norm_mix, norm_ffn, norm_final,
           w_in_a, conv_w_a, a_log, dt_bias, norm_o_a, w_out_a, w_in_b, w_out_b,
           w_gate_d, w_up_d, w_down_d, w_router, b_router, w_gate_e, w_up_e, w_down_e):
    Bp, T, D = x_prompt.shape
    Bs, Ts, _ = x_sample.shape
    assert Ts == 1 and cache_kv_w128.shape[2] == DIL_GROUPS[0][0]
    p = dict(norm_mix=norm_mix, norm_ffn=norm_ffn, norm_final=norm_final, w_in_a=w_in_a, conv_w_a=conv_w_a,
             a_log=a_log, dt_bias=dt_bias, norm_o_a=norm_o_a, w_out_a=w_out_a, w_in_b=w_in_b, w_out_b=w_out_b,
             w_gate_d=w_gate_d, w_up_d=w_up_d, w_down_d=w_down_d, w_router=w_router, b_router=b_router,
             w_gate_e=w_gate_e, w_up_e=w_up_e, w_down_e=w_down_e)

    n_c = Bp + Bs
    pad = (-n_c) % 8
    c_all = jnp.concatenate([c_prompt, c_sample, jnp.zeros((pad, D), F32)], axis=0)
    mods = [_mm(c_all, w_ada, (l,), 6 * D, out_dtype=F32, pre_silu=True, res=b_ada[l].reshape(1, 6 * D),
                rows_per_group=n_c + pad, name="adaln") for l in range(DEPTH)]
    mod_p = jnp.stack([m[:Bp] for m in mods])
    mod_s = jnp.stack([m[Bp:n_c] for m in mods])

    half = ATT_DH // 2
    inv_freq = ROPE_THETA ** (-jnp.arange(half, dtype=F32) / half)
    invf = jnp.tile(inv_freq, LANES // half).reshape(1, LANES)

    y_p, s_p, cv_p, kv0_p, kv1_p, kv2_p = _trunk(
        x_prompt.reshape(Bp * T, D), mod_p, T, Bp, T, True, None, None, None, invf, p)
    y_s, s_s, cv_s, kv0_s, kv1_s, kv2_s = _trunk(
        x_sample.reshape(Bs, D), mod_s, 1, Bs, 1, False, state_gdn,
        jnp.transpose(state_conv, (0, 2, 1, 3)), (cache_kv_w128, cache_kv_w512, cache_kv_w2048), invf, p)
    return (y_p.reshape(Bp, T, D), y_s.reshape(Bs, Ts, D), s_p, cv_p, kv0_p, kv1_p, kv2_p,
            s_s, cv_s, kv0_s, kv1_s, kv2_s)
```

```python
import functools
import math

import jax
import jax.numpy as jnp
from jax import lax
from jax.experimental import pallas as pl
from jax.experimental.pallas import tpu as pltpu

F32 = jnp.float32
BF16 = jnp.bfloat16

DEPTH = 4
GDN_HK = 8
GDN_HV = 16
GDN_DK = 128
GDN_DV = 128
GDN_KEY_DIM = GDN_HK * GDN_DK
GDN_VAL_DIM = GDN_HV * GDN_DV
GDN_QKV_DIM = 2 * GDN_KEY_DIM + GDN_VAL_DIM
CONV_W = 4
GDN_CHUNK = 64
ATT_H = 16
ATT_DH = 64
ATT_DIM = ATT_H * ATT_DH
DIL_GROUPS = ((128, 1), (512, 4), (2048, 16))
N_GROUPS = len(DIL_GROUPS)
ATT_BLOCK = 128
ROPE_THETA = 10000.0
N_EXPERTS = 8
EPS = 1e-6
NEG_INF = -1e30
PAST_LEN = 2048

V7X_VMEM_BYTES = 64 * 1024 * 1024
VMEM_LIMIT_BYTES = 52 * 1024 * 1024
VMEM_BLOCK_BUDGET = 30 * 1024 * 1024
LANES = 128


def _cparams(sem):
    return pltpu.CompilerParams(dimension_semantics=sem, vmem_limit_bytes=VMEM_LIMIT_BYTES)


def _dot(a, b):
    return jnp.dot(a.astype(BF16), b.astype(BF16), preferred_element_type=F32)


def _dot_nt(a, b):
    return lax.dot_general(a.astype(BF16), b.astype(BF16), (((1,), (1,)), ((), ())),
                           preferred_element_type=F32)


def _dot_tn(a, b):
    return lax.dot_general(a.astype(BF16), b.astype(BF16), (((0,), (0,)), ((), ())),
                           preferred_element_type=F32)


def _split2(a):
    hi = a.astype(BF16)
    lo = (a - hi.astype(F32)).astype(BF16)
    return hi, lo


def _split3(a):
    hi = a.astype(BF16)
    r = a - hi.astype(F32)
    mid = r.astype(BF16)
    lo = (r - mid.astype(F32)).astype(BF16)
    return hi, mid, lo


def _dot3(a, b):
    ah, al = _split2(a)
    bh, bl = _split2(b)
    return _dot(ah, bh) + (_dot(ah, bl) + _dot(al, bh))


def _dot_exact_lhs(c, b, passes=3):
    parts = _split3(b) if passes == 3 else _split2(b)
    acc = _dot(c, parts[0])
    for p in parts[1:]:
        acc = acc + _dot(c, p)
    return acc


def _dot_exact_rhs(a, c, passes=3):
    parts = _split3(a) if passes == 3 else _split2(a)
    acc = _dot(parts[0], c)
    for p in parts[1:]:
        acc = acc + _dot(p, c)
    return acc


def _silu(x):
    return x * jax.nn.sigmoid(x)


def _iota(shape, dim):
    return lax.broadcasted_iota(jnp.int32, shape, dim)


def _norm_mod_body(x_ref, nw_ref, *rest, modulate):
    if modulate:
        sh_ref, sc_ref, o_ref = rest
    else:
        (o_ref,) = rest
    x = x_ref[...].astype(F32)
    y = x * lax.rsqrt(jnp.mean(x * x, axis=-1, keepdims=True) + EPS) * nw_ref[...]
    if modulate:
        y = y * (1.0 + sc_ref[...]) + sh_ref[...]
    o_ref[...] = y.astype(o_ref.dtype)


def _norm_mod(x, nw, shift, scale, rows_per_group, out_dtype):
    M, D = x.shape
    tm = min(M, 1024)
    assert M % tm == 0
    modulate = shift is not None
    in_specs = [pl.BlockSpec((tm, D), lambda i: (i, 0)), pl.BlockSpec((1, D), lambda i: (0, 0))]
    args = [x, nw.reshape(1, D)]
    if modulate:
        if rows_per_group == 1:
            spec = pl.BlockSpec((tm, D), lambda i: (i, 0))
            args += [shift, scale]
        else:
            assert rows_per_group % tm == 0
            spec = pl.BlockSpec((None, 1, D), lambda i: ((i * tm) // rows_per_group, 0, 0))
            args += [shift.reshape(-1, 1, D), scale.reshape(-1, 1, D)]
        in_specs += [spec, spec]
    return pl.pallas_call(
        functools.partial(_norm_mod_body, modulate=modulate),
        out_shape=jax.ShapeDtypeStruct((M, D), out_dtype),
        grid=(M // tm,),
        in_specs=in_specs,
        out_specs=pl.BlockSpec((tm, D), lambda i: (i, 0)),
        compiler_params=_cparams(("parallel",)),
        name="norm_mod",
    )(*args)


def _mm_body(*refs, pre_silu, swiglu, has_rowscale, rowscale_col, has_res, has_gate, has_res2):
    it = iter(refs)
    x_ref = next(it)
    w_ref = next(it)
    w2_ref = next(it) if swiglu else None
    rs_ref = next(it) if has_rowscale else None
    res_ref = next(it) if has_res else None
    gate_ref = next(it) if has_gate else None
    res2_ref = next(it) if has_res2 else None
    o_ref = next(it)
    x = x_ref[...]
    if pre_silu:
        x = _silu(x.astype(F32))
    xb = x.astype(BF16)
    acc = jnp.dot(xb, w_ref[...].astype(BF16), preferred_element_type=F32)
    if swiglu:
        acc = _silu(acc) * jnp.dot(xb, w2_ref[...].astype(BF16), preferred_element_type=F32)
    if has_rowscale:
        acc = acc * rs_ref[...][:, rowscale_col:rowscale_col + 1]
    if has_res:
        acc = acc + res_ref[...]
    if has_gate:
        acc = acc * gate_ref[...]
    if has_res2:
        acc = acc + res2_ref[...]
    o_ref[...] = acc.astype(o_ref.dtype)


def _pick_tiles(M, K, N, x_bytes, n_w, out_bytes, n_res):
    tm = min(M, 1024)
    while M % tm:
        tm //= 2
    cands = [t for t in (1024, 896, 768, 512, 384, 256, 128) if N % t == 0]
    if not cands:
        cands = [N]

    def est(tm_, tn_):
        blocks = tm_ * K * x_bytes + n_w * K * tn_ * 4 + tm_ * tn_ * (out_bytes + 4 * n_res)
        temps = n_w * K * tn_ * 2 + tm_ * K * 2 + (1 + n_w) * tm_ * tn_ * 4
        return 2 * blocks + temps

    ci = 0
    while est(tm, cands[ci]) > VMEM_BLOCK_BUDGET:
        if ci + 1 < len(cands) and cands[ci] > 256:
            ci += 1
        elif tm > 256 and M % (tm // 2) == 0:
            tm //= 2
        elif ci + 1 < len(cands):
            ci += 1
        else:
            break
    return tm, cands[ci]


def _mm(x, w, lead, n_cols, col0=0, *, out_dtype, w2=None, pre_silu=False, rowscale=None,
        rowscale_col=0, res=None, gate=None, res2=None, rows_per_group=1, name="mm"):
    M, K = x.shape
    swiglu = w2 is not None
    n_res = int(res is not None) + int(res2 is not None) + int(gate is not None)
    tm, tn = _pick_tiles(M, K, n_cols, x.dtype.itemsize, 2 if swiglu else 1,
                         jnp.dtype(out_dtype).itemsize, n_res)
    assert col0 % tn == 0 and n_cols % tn == 0
    nj, ni = n_cols // tn, M // tm
    jb0 = col0 // tn
    nlead = len(lead)
    wblock = (None,) * nlead + (K, tn)

    def wmap(j, i):
        return tuple(lead) + (0, jb0 + j)

    in_specs = [pl.BlockSpec((tm, K), lambda j, i: (i, 0)), pl.BlockSpec(wblock, wmap)]
    args = [x, w]
    if swiglu:
        in_specs.append(pl.BlockSpec(wblock, wmap))
        args.append(w2)
    if rowscale is not None:
        in_specs.append(pl.BlockSpec((tm, rowscale.shape[1]), lambda j, i: (i, 0)))
        args.append(rowscale)

    def tile_spec(a):
        if a.shape[0] == M:
            return pl.BlockSpec((tm, tn), lambda j, i: (i, j)), a
        assert rows_per_group % tm == 0 and a.shape[0] * rows_per_group == M
        return (pl.BlockSpec((None, 1, tn), lambda j, i: ((i * tm) // rows_per_group, 0, j)),
                a.reshape(a.shape[0], 1, a.shape[1]))

    for a in (res, gate, res2):
        if a is not None:
            s, a2 = tile_spec(a)
            in_specs.append(s)
            args.append(a2)
    body = functools.partial(
        _mm_body, pre_silu=pre_silu, swiglu=swiglu, has_rowscale=rowscale is not None,
        rowscale_col=rowscale_col, has_res=res is not None, has_gate=gate is not None,
        has_res2=res2 is not None)
    return pl.pallas_call(
        body,
        out_shape=jax.ShapeDtypeStruct((M, n_cols), out_dtype),
        grid=(nj, ni),
        in_specs=in_specs,
        out_specs=pl.BlockSpec((tm, tn), lambda j, i: (i, j)),
        compiler_params=_cparams(("parallel", "parallel")),
        name=name,
    )(*args)


def _mm_grouped(x, w, layer, tile_expert, tm, *, out_dtype, w2=None, name="mm_grouped"):
    M, K = x.shape
    N = w.shape[-1]
    swiglu = w2 is not None
    tn = next(t for t in (896, 512, 256, 128) if N % t == 0)
    nj, ni = N // tn, M // tm
    wspec = pl.BlockSpec((None, None, K, tn), lambda j, i, te: (layer, te[i], 0, j))
    in_specs = [pl.BlockSpec((tm, K), lambda j, i, te: (i, 0)), wspec]
    args = [x, w]
    if swiglu:
        in_specs.append(wspec)
        args.append(w2)
    inner = functools.partial(_mm_body, pre_silu=False, swiglu=swiglu, has_rowscale=False, rowscale_col=0,
                              has_res=False, has_gate=False, has_res2=False)

    def body(te_ref, *refs):
        inner(*refs)

    return pl.pallas_call(
        body,
        out_shape=jax.ShapeDtypeStruct((M, N), out_dtype),
        grid_spec=pltpu.PrefetchScalarGridSpec(
            num_scalar_prefetch=1, grid=(nj, ni), in_specs=in_specs,
            out_specs=pl.BlockSpec((tm, tn), lambda j, i, te: (i, j))),
        compiler_params=_cparams(("parallel", "parallel")),
        name=name,
    )(tile_expert, *args)


def _mm_small(x, w_small, name):
    M, K = x.shape
    n = w_small.shape[1]
    tm = min(M, 1024)

    def body(x_ref, w_ref, o_ref):
        o_ref[...] = _dot(x_ref[...], w_ref[...])

    return pl.pallas_call(
        body,
        out_shape=jax.ShapeDtypeStruct((M, n), F32),
        grid=(M // tm,),
        in_specs=[pl.BlockSpec((tm, K), lambda i: (i, 0)), pl.BlockSpec((K, n), lambda i: (0, 0))],
        out_specs=pl.BlockSpec((tm, n), lambda i: (i, 0)),
        compiler_params=_cparams(("parallel",)),
        name=name,
    )(x, w_small)


GDN_PREP_ROWS = 512
GDN_PREP_CHUNKS = GDN_PREP_ROWS // GDN_CHUNK
PAIR = 2 * GDN_CHUNK


def _gdn_gates(ab, alog, dtb):
    a = ab[:, :GDN_HV]
    b = ab[:, GDN_HV:]
    xg = a + dtb
    softplus = jnp.maximum(xg, 0.0) + jnp.log1p(jnp.exp(-jnp.abs(xg)))
    return -jnp.exp(alog) * softplus, jax.nn.sigmoid(b)


def _gdn_prep_body(q_ref, qp_ref, k_ref, kp_ref, v_ref, vp_ref, cwq_ref, cwk_ref, cwv_ref,
                   ab_ref, alog_ref, dtb_ref,
                   u_ref, w_ref, qd_ref, kd_ref, qk_ref, gl_ref):
    i = pl.program_id(1)
    hk = pl.program_id(2)
    rows = GDN_PREP_ROWS
    first = i == 0

    def conv(x_ref, xp_ref, cw_ref):
        cur = x_ref[...]
        prev = jnp.where(first, 0.0, xp_ref[...])
        ext = jnp.concatenate([prev, cur], axis=0)
        cw = cw_ref[...]
        y = cur * cw[CONV_W - 1:CONV_W, :]
        for s in range(1, CONV_W):
            y = y + ext[8 - s:8 - s + rows, :] * cw[CONV_W - 1 - s:CONV_W - s, :]
        return _silu(y)

    qc = conv(q_ref, qp_ref, cwq_ref)
    kc = conv(k_ref, kp_ref, cwk_ref)
    vc = conv(v_ref, vp_ref, cwv_ref)
    qn = qc * lax.rsqrt(jnp.sum(qc * qc, axis=-1, keepdims=True) + EPS) * (GDN_DK ** -0.5)
    kn = kc * lax.rsqrt(jnp.sum(kc * kc, axis=-1, keepdims=True) + EPS)

    g16, be16 = _gdn_gates(ab_ref[...], alog_ref[...], dtb_ref[...])
    r16 = _iota((GDN_HV, LANES), 0)

    def head_bcast(x16, hv):
        return _dot_exact_rhs(x16, (r16 == hv).astype(BF16))

    g_b = [head_bcast(g16, 2 * hk + s) for s in (0, 1)]
    be_b = [head_bcast(be16, 2 * hk + s) for s in (0, 1)]

    ri = _iota((PAIR, PAIR), 0)
    ci = _iota((PAIR, PAIR), 1)
    same = (ri >> 6) == (ci >> 6)
    causal = same & (ri >= ci)
    strict = same & (ri > ci)
    blk16 = (ri >> 4) == (ci >> 4)
    tri = causal.astype(BF16)
    eye = (ri == ci).astype(F32)
    r8 = _iota((8, LANES), 0)

    chunks = range(GDN_PREP_CHUNKS)
    sls = [slice(c * GDN_CHUNK, (c + 1) * GDN_CHUNK) for c in chunks]
    k2 = [jnp.concatenate([kn[sl], kn[sl]], axis=0) for sl in sls]
    q2 = [jnp.concatenate([qn[sl], qn[sl]], axis=0) for sl in sls]
    g2 = [jnp.concatenate([g_b[0][sl], g_b[1][sl]], axis=0) for sl in sls]
    b2 = [jnp.concatenate([be_b[0][sl], be_b[1][sl]], axis=0) for sl in sls]

    res = [_dot_exact_lhs(tri, jnp.concatenate([jnp.where(strict, g, 0.0), g], axis=1)) for g in g2]
    kk = [_dot_nt(k, k) for k in k2]
    qkr = [_dot_nt(q, k) for q, k in zip(q2, k2)]
    gcum = [r[:, PAIR:] for r in res]
    decay = [jnp.where(causal, jnp.exp(jnp.where(causal, r[:, :PAIR], 0.0)), 0.0) for r in res]
    eg = [jnp.exp(g) for g in gcum]
    lmat = [jnp.where(strict, b * k * d, 0.0) for b, k, d in zip(b2, kk, decay)]

    ld = [jnp.where(blk16, l, 0.0) for l in lmat]
    nn = [jnp.where(blk16, 0.0, l) for l in lmat]
    x = [eye - l for l in ld]
    p = [_dot(l, l) for l in ld]
    for _ in range(2):
        x = [xi + _dot(xi, pi) for xi, pi in zip(x, p)]
        p = [_dot(pi, pi) for pi in p]
    x = [xi + _dot(xi, pi) for xi, pi in zip(x, p)]
    v2 = [jnp.concatenate([vc[sl, :GDN_DV], vc[sl, GDN_DV:]], axis=0) for sl in sls]
    rhs = [jnp.concatenate([b * v, b * e * k], axis=1) for b, v, e, k in zip(b2, v2, eg, k2)]
    mm = [_dot(xi, n) for xi, n in zip(x, nn)]
    y = [_dot(xi, r) for xi, r in zip(x, rhs)]
    m2 = [_dot(m, m) for m in mm]
    y = [yi + _dot(m, yi) for yi, m in zip(y, m2)]
    y = [yi - _dot(m, yi) for yi, m in zip(y, mm)]

    for c in chunks:
        gl0 = gcum[c][GDN_CHUNK - 1:GDN_CHUNK, :]
        gl1 = gcum[c][PAIR - 1:PAIR, :]
        glast = jnp.where(ri < GDN_CHUNK, gl0, gl1)
        u_ref[c] = y[c][:, :GDN_DV]
        w_ref[c] = y[c][:, GDN_DV:].astype(BF16)
        qd_ref[c] = (q2[c] * eg[c]).astype(BF16)
        kd_ref[c] = (k2[c] * jnp.exp(glast - gcum[c])).astype(BF16)
        qk_ref[c] = (qkr[c] * decay[c]).astype(BF16)
        gl_ref[c] = jnp.where(r8 == 0, jnp.exp(gl0), jnp.where(r8 == 1, jnp.exp(gl1), 0.0))


def _gdn_prep(proj, ab, conv_w, a_log, dt_bias, layer, Bn, T):
    rows = GDN_PREP_ROWS
    nblk = T // rows
    nch = T // GDN_CHUNK
    cpb = GDN_PREP_CHUNKS

    def cur(width, col):
        return pl.BlockSpec((rows, width), lambda b, i, h: (b * nblk + i, col(h)))

    def prev(width, col):
        return pl.BlockSpec((8, width),
                            lambda b, i, h: (jnp.maximum((b * T + i * rows) // 8 - 1, 0), col(h)))

    def cw(width, col):
        return pl.BlockSpec((None, CONV_W, width), lambda b, i, h: (layer, 0, col(h)))

    qcol = lambda h: h
    kcol = lambda h: GDN_HK + h
    vcol = lambda h: GDN_HK + h
    in_specs = [cur(128, qcol), prev(128, qcol), cur(128, kcol), prev(128, kcol),
                cur(256, vcol), prev(256, vcol),
                cw(128, qcol), cw(128, kcol), cw(256, vcol),
                pl.BlockSpec((rows, 2 * GDN_HV), lambda b, i, h: (b * nblk + i, 0)),
                pl.BlockSpec((None, 1, GDN_HV), lambda b, i, h: (layer, 0, 0)),
                pl.BlockSpec((None, 1, GDN_HV), lambda b, i, h: (layer, 0, 0))]
    big = lambda dt: jax.ShapeDtypeStruct((Bn, GDN_HK, nch, PAIR, LANES), dt)
    ospec = pl.BlockSpec((None, None, cpb, PAIR, LANES), lambda b, i, h: (b, h, i, 0, 0))
    out_shape = (big(F32), big(BF16), big(BF16), big(BF16), big(BF16),
                 jax.ShapeDtypeStruct((Bn, GDN_HK, nch, 8, LANES), F32))
    out_specs = (ospec, ospec, ospec, ospec, ospec,
                 pl.BlockSpec((None, None, cpb, 8, LANES), lambda b, i, h: (b, h, i, 0, 0)))
    return pl.pallas_call(
        _gdn_prep_body,
        out_shape=out_shape,
        grid=(Bn, nblk, GDN_HK),
        in_specs=in_specs,
        out_specs=out_specs,
        compiler_params=_cparams(("parallel", "parallel", "parallel")),
        name="gdn_prep",
    )(proj, proj, proj, proj, proj, proj, conv_w, conv_w, conv_w, ab,
      a_log.reshape(-1, 1, GDN_HV), dt_bias.reshape(-1, 1, GDN_HV))


GDN_SCAN_ROWS = 256
GDN_SCAN_CHUNKS = GDN_SCAN_ROWS // GDN_CHUNK


def _gated_norm_store(o_scr, z_ref, no_ref, og_ref):
    for h in range(GDN_HV):
        sl = slice(h * GDN_DV, (h + 1) * GDN_DV)
        oh = o_scr[:, sl]
        zh = z_ref[:, sl]
        on = oh * lax.rsqrt(jnp.mean(oh * oh, axis=-1, keepdims=True) + EPS) * no_ref[...]
        og_ref[:, sl] = (on * _silu(zh)).astype(og_ref.dtype)


def _gdn_scan_body(u_ref, w_ref, qd_ref, kd_ref, qk_ref, gl_ref, z_ref, no_ref,
                   og_ref, s_ref, o_scr):
    @pl.when(pl.program_id(1) == 0)
    def _():
        s_ref[...] = jnp.zeros_like(s_ref)

    C = GDN_CHUNK
    heads = range(GDN_HV)
    for c in range(GDN_SCAN_CHUNKS):
        half = lambda ref, hv: ref[hv // 2, c][(hv % 2) * C:(hv % 2 + 1) * C]
        s = [s_ref[hv] for hv in heads]
        ws = [_dot(half(w_ref, hv), s[hv]) for hv in heads]
        qs = [_dot(half(qd_ref, hv), s[hv]) for hv in heads]
        vn = [half(u_ref, hv) - ws[hv] for hv in heads]
        oi = [_dot(qk_ref[hk, c], jnp.concatenate([vn[2 * hk], vn[2 * hk + 1]], axis=0))
              for hk in range(GDN_HK)]
        kv = [_dot_tn(half(kd_ref, hv), vn[hv]) for hv in heads]
        for hv in heads:
            gl = gl_ref[hv // 2, c]
            s_ref[hv] = s[hv] * gl[hv % 2:hv % 2 + 1, :] + kv[hv]
            o_scr[c * C:(c + 1) * C, hv * GDN_DV:(hv + 1) * GDN_DV] = (
                qs[hv] + oi[hv // 2][(hv % 2) * C:(hv % 2 + 1) * C])
    _gated_norm_store(o_scr, z_ref, no_ref, og_ref)


def _gdn_scan(prep, proj, norm_o, layer, Bn, T):
    u, w, qd, kd, qk, gl = prep
    rows = GDN_SCAN_ROWS
    nblk = T // rows
    cpb = GDN_SCAN_CHUNKS
    bspec = pl.BlockSpec((None, GDN_HK, cpb, PAIR, LANES), lambda b, i: (b, 0, i, 0, 0))
    in_specs = [bspec, bspec, bspec, bspec, bspec,
                pl.BlockSpec((None, GDN_HK, cpb, 8, LANES), lambda b, i: (b, 0, i, 0, 0)),
                pl.BlockSpec((rows, GDN_VAL_DIM), lambda b, i: (b * nblk + i, GDN_QKV_DIM // GDN_VAL_DIM)),
                pl.BlockSpec((None, 1, GDN_DV), lambda b, i: (layer, 0, 0))]
    out_shape = (jax.ShapeDtypeStruct((Bn * T, GDN_VAL_DIM), BF16),
                 jax.ShapeDtypeStruct((Bn, GDN_HV, GDN_DK, GDN_DV), F32))
    out_specs = (pl.BlockSpec((rows, GDN_VAL_DIM), lambda b, i: (b * nblk + i, 0)),
                 pl.BlockSpec((None, GDN_HV, GDN_DK, GDN_DV), lambda b, i: (b, 0, 0, 0)))
    return pl.pallas_call(
        _gdn_scan_body,
        out_shape=out_shape,
        grid=(Bn, nblk),
        in_specs=in_specs,
        out_specs=out_specs,
        scratch_shapes=[pltpu.VMEM((rows, GDN_VAL_DIM), F32)],
        compiler_params=_cparams(("parallel", "arbitrary")),
        name="gdn_scan",
    )(u, w, qd, kd, qk, gl, proj, norm_o.reshape(-1, 1, GDN_DV))


GDN_STEP_ROWS = 8


def _gdn_step_body(proj_ref, ab_ref, cb_ref, cw_ref, alog_ref, dtb_ref, no_ref, st_ref,
                   og_ref, cbo_ref, sto_ref, o_scr):
    R = GDN_STEP_ROWS
    qkv = proj_ref[:, :GDN_QKV_DIM]
    cw = cw_ref[...]
    y = cb_ref[0] * cw[0:1, :]
    y = y + cb_ref[1] * cw[1:2, :]
    y = y + cb_ref[2] * cw[2:3, :]
    y = y + qkv * cw[3:4, :]
    qkv_c = _silu(y)
    cbo_ref[0] = cb_ref[1]
    cbo_ref[1] = cb_ref[2]
    cbo_ref[2] = qkv

    g16, be16 = _gdn_gates(ab_ref[...], alog_ref[...], dtb_ref[...])
    expand = ((_iota((GDN_HV, GDN_VAL_DIM), 1) >> 7) == _iota((GDN_HV, GDN_VAL_DIM), 0)).astype(BF16)
    eg_f = jnp.exp(_dot_exact_rhs(g16, expand))
    be_f = _dot_exact_rhs(be16, expand)
    r8 = _iota((8, LANES), 0)

    for hk in range(GDN_HK):
        qh = qkv_c[:, hk * GDN_DK:(hk + 1) * GDN_DK]
        kh = qkv_c[:, GDN_KEY_DIM + hk * GDN_DK:GDN_KEY_DIM + (hk + 1) * GDN_DK]
        qn = qh * lax.rsqrt(jnp.sum(qh * qh, axis=-1, keepdims=True) + EPS) * (GDN_DK ** -0.5)
        kn = kh * lax.rsqrt(jnp.sum(kh * kh, axis=-1, keepdims=True) + EPS)
        qk = jnp.sum(qn * kn, axis=-1, keepdims=True)
        for s in (0, 1):
            hv = 2 * hk + s
            sl = slice(hv * GDN_DV, (hv + 1) * GDN_DV)
            eg = eg_f[:, sl]
            be = be_f[:, sl]
            vh = qkv_c[:, 2 * GDN_KEY_DIM + hv * GDN_DV:2 * GDN_KEY_DIM + (hv + 1) * GDN_DV]
            w_rows = be * eg * kn
            qd_rows = qn * eg
            rows = range(R)
            st = [st_ref[bb, hv] for bb in rows]
            rr = [_dot(jnp.where(r8 == 0, w_rows[bb:bb + 1], jnp.where(r8 == 1, qd_rows[bb:bb + 1], 0.0)),
                       st[bb]) for bb in rows]
            vn = [be[bb:bb + 1] * vh[bb:bb + 1] - rr[bb][0:1] for bb in rows]
            kv = [_dot_tn(jnp.where(r8 == 0, kn[bb:bb + 1], 0.0), jnp.where(r8 == 0, vn[bb], 0.0))
                  for bb in rows]
            for bb in rows:
                sto_ref[bb, hv] = st[bb] * eg[bb:bb + 1] + kv[bb]
                o_scr[bb:bb + 1, sl] = rr[bb][1:2] + qk[bb:bb + 1] * vn[bb]
    _gated_norm_store(o_scr, proj_ref.at[:, GDN_QKV_DIM:], no_ref, og_ref)


def _gdn_step(proj, ab, conv_buf_t, conv_w, a_log, dt_bias, norm_o, state, layer):
    Bs = proj.shape[0]
    R = GDN_STEP_ROWS
    vec = lambda n: pl.BlockSpec((None, 1, n), lambda i: (layer, 0, 0))
    in_specs = [pl.BlockSpec((R, proj.shape[1]), lambda i: (i, 0)),
                pl.BlockSpec((R, 2 * GDN_HV), lambda i: (i, 0)),
                pl.BlockSpec((None, CONV_W - 1, R, GDN_QKV_DIM), lambda i: (layer, 0, i, 0)),
                pl.BlockSpec((None, CONV_W, GDN_QKV_DIM), lambda i: (layer, 0, 0)),
                vec(GDN_HV), vec(GDN_HV), vec(GDN_DV),
                pl.BlockSpec((None, R, GDN_HV, GDN_DK, GDN_DV), lambda i: (layer, i, 0, 0, 0))]
    out_shape = (jax.ShapeDtypeStruct((Bs, GDN_VAL_DIM), F32),
                 jax.ShapeDtypeStruct((CONV_W - 1, Bs, GDN_QKV_DIM), F32),
                 jax.ShapeDtypeStruct((Bs, GDN_HV, GDN_DK, GDN_DV), F32))
    out_specs = (pl.BlockSpec((R, GDN_VAL_DIM), lambda i: (i, 0)),
                 pl.BlockSpec((CONV_W - 1, R, GDN_QKV_DIM), lambda i: (0, i, 0)),
                 pl.BlockSpec((R, GDN_HV, GDN_DK, GDN_DV), lambda i: (i, 0, 0, 0)))
    return pl.pallas_call(
        _gdn_step_body,
        out_shape=out_shape,
        grid=(Bs // R,),
        in_specs=in_specs,
        out_specs=out_specs,
        scratch_shapes=[pltpu.VMEM((R, GDN_VAL_DIM), F32)],
        compiler_params=_cparams(("parallel",)),
        name="gdn_step",
    )(proj, ab, conv_buf_t, conv_w, a_log.reshape(-1, 1, GDN_HV), dt_bias.reshape(-1, 1, GDN_HV),
      norm_o.reshape(-1, 1, GDN_DV), state)


def _rope_tables(ang):
    lane = _iota(ang.shape, 1)
    cs = jnp.cos(ang)
    sn = jnp.sin(ang)
    sg = jnp.where((lane & (ATT_DH - 1)) < ATT_DH // 2, -sn, sn)
    reps = ATT_DIM // LANES
    return jnp.concatenate([cs] * reps, axis=1), jnp.concatenate([sg] * reps, axis=1)


def _rope_apply(x, cs, sg):
    lane = _iota(x.shape, 1)
    first_half = (lane & (ATT_DH - 1)) < ATT_DH // 2
    half = ATT_DH // 2
    partner = jnp.where(first_half, pltpu.roll(x, x.shape[1] - half, 1), pltpu.roll(x, half, 1))
    return x * cs + partner * sg


def _rope_body(invf_ref, x_ref, o_ref, *, pos0, pos_step, row0, tr, col0):
    i = pl.program_id(1)
    c = pl.program_id(2)
    is_v = lax.rem(col0 + c, 3) == 2

    @pl.when(is_v)
    def _():
        o_ref[...] = x_ref[...].astype(o_ref.dtype)

    @pl.when(jnp.logical_not(is_v))
    def _():
        pos = (pos0 + pos_step * (row0 + i * tr + _iota((tr, LANES), 0))).astype(F32)
        cs, sg = _rope_tables(pos * invf_ref[...])
        o_ref[...] = _rope_apply(x_ref[...], cs, sg).astype(o_ref.dtype)


def _rope(proj, invf, Bn, T, row0, nrows, col0, ncols, out_dtype, pos0=0, pos_step=1):
    tr = min(nrows, 256)
    assert nrows % tr == 0 and row0 % tr == 0 and T % tr == 0
    nb = nrows // tr
    body = functools.partial(_rope_body, pos0=pos0, pos_step=pos_step, row0=row0, tr=tr, col0=col0)
    return pl.pallas_call(
        body,
        out_shape=jax.ShapeDtypeStruct((Bn * nrows, ncols * ATT_DIM), out_dtype),
        grid=(Bn, nb, ncols),
        in_specs=[pl.BlockSpec((1, LANES), lambda b, i, c: (0, 0)),
                  pl.BlockSpec((tr, ATT_DIM), lambda b, i, c: (b * (T // tr) + row0 // tr + i, col0 + c))],
        out_specs=pl.BlockSpec((tr, ATT_DIM), lambda b, i, c: (b * nb + i, c)),
        compiler_params=_cparams(("parallel", "parallel", "parallel")),
        name="rope",
    )(invf, proj)


def _attn_body(q_ref, kp_ref, kc_ref, vp_ref, vc_ref, o_ref, lse_ref):
    n = pl.program_id(2)
    blk = ATT_BLOCK
    qi = _iota((blk, 2 * blk), 0)
    kj = _iota((blk, 2 * blk), 1)
    dist = qi + blk - kj
    n_win = blk
    mask = (dist >= 0) & (dist <= n_win) & ((kj >= blk) | (n > 0))
    lane = _iota((blk, LANES), 1)
    lo = lane < ATT_DH
    lse_acc = jnp.zeros((blk, LANES), F32)
    scale = ATT_DH ** -0.5
    for hp in range(ATT_H // 2):
        sl = slice(hp * LANES, (hp + 1) * LANES)
        q2 = q_ref[:, sl].astype(F32)
        k2 = jnp.concatenate([kp_ref[:, sl], kc_ref[:, sl]], axis=0)
        v2 = jnp.concatenate([vp_ref[:, sl], vc_ref[:, sl]], axis=0)
        outs = []
        for s in (0, 1):
            qm = jnp.where(lo if s == 0 else jnp.logical_not(lo), q2, 0.0)
            sc = _dot_nt(qm, k2) * scale
            sc = jnp.where(mask, sc, NEG_INF)
            m = jnp.max(sc, axis=-1, keepdims=True)
            p = jnp.exp(sc - m)
            l = jnp.sum(p, axis=-1, keepdims=True)
            outs.append(_dot(p, v2) * (1.0 / l))
            lse_acc = jnp.where(lane == 2 * hp + s, m + jnp.log(l), lse_acc)
        o_ref[:, sl] = jnp.where(lo, outs[0], outs[1])
    lse_ref[...] = lse_acc


def _attn_prompt(qkv, gi, Bn, T):
    dil = DIL_GROUPS[gi][1]
    Td = T // dil
    nb = Td // ATT_BLOCK

    def spec(which, prev):
        def imap(b, r, n):
            return (b, r, jnp.maximum(n - 1, 0) if prev else n, which)
        return pl.BlockSpec((None, None, ATT_BLOCK, ATT_DIM), imap)

    return pl.pallas_call(
        _attn_body,
        out_shape=(jax.ShapeDtypeStruct((Bn, dil, Td, ATT_DIM), F32),
                   jax.ShapeDtypeStruct((Bn, dil, Td, LANES), F32)),
        grid=(Bn, dil, nb),
        in_specs=[spec(0, False), spec(1, True), spec(1, False), spec(2, True), spec(2, False)],
        out_specs=(pl.BlockSpec((None, None, ATT_BLOCK, ATT_DIM), lambda b, r, n: (b, r, n, 0)),
                   pl.BlockSpec((None, None, ATT_BLOCK, LANES), lambda b, r, n: (b, r, n, 0))),
        compiler_params=_cparams(("parallel", "parallel", "parallel")),
        name="attn_prompt",
    )(qkv, qkv, qkv, qkv, qkv)


ROPE_SPLIT_ROWS = 256


def _stride_perm(rows, dil, transpose=False):
    n = rows // dil
    i = _iota((rows, rows), 1 if transpose else 0)
    j = _iota((rows, rows), 0 if transpose else 1)
    shift = n.bit_length() - 1
    assert n == 1 << shift
    return (j == (i & (n - 1)) * dil + (i >> shift)).astype(BF16)


def _rope_split_body(invf_ref, x_ref, o0_ref, o1_ref, o2_ref):
    rows = ROPE_SPLIT_ROWS
    pos = (pl.program_id(1) * rows + _iota((rows, LANES), 0)).astype(F32)
    cs, sg = _rope_tables(pos * invf_ref[...])
    for gi, o_ref in enumerate((o0_ref, o1_ref, o2_ref)):
        dil = DIL_GROUPS[gi][1]
        n = rows // dil
        perm = _stride_perm(rows, dil) if dil > 1 else None
        for which in range(3):
            xx = x_ref[:, (3 * gi + which) * ATT_DIM:(3 * gi + which + 1) * ATT_DIM]
            if which < 2:
                xx = _rope_apply(xx, cs, sg)
            xb = xx.astype(BF16)
            if perm is not None:
                xb = jnp.dot(perm, xb, preferred_element_type=F32).astype(BF16)
            for r in range(dil):
                o_ref[r, :, which * ATT_DIM:(which + 1) * ATT_DIM] = xb[r * n:(r + 1) * n, :]


def _rope_split(proj, invf, Bn, T):
    rows = ROPE_SPLIT_ROWS
    nblk = T // rows
    out_shape, out_specs = [], []
    for _, dil in DIL_GROUPS:
        out_shape.append(jax.ShapeDtypeStruct((Bn, dil, T // dil, 3 * ATT_DIM), BF16))
        out_specs.append(pl.BlockSpec((None, dil, rows // dil, 3 * ATT_DIM), lambda b, i: (b, 0, i, 0)))
    return pl.pallas_call(
        _rope_split_body,
        out_shape=tuple(out_shape),
        grid=(Bn, nblk),
        in_specs=[pl.BlockSpec((1, LANES), lambda b, i: (0, 0)),
                  pl.BlockSpec((rows, 3 * N_GROUPS * ATT_DIM), lambda b, i: (b * nblk + i, 0))],
        out_specs=tuple(out_specs),
        compiler_params=_cparams(("parallel", "parallel")),
        name="rope_split",
    )(invf, proj)


MERGE_ROWS = 256


def _merge_body(o0_ref, o1_ref, o2_ref, l0_ref, l1_ref, l2_ref, out_ref):
    rows = MERGE_ROWS

    def position_order(ref, gi, passes):
        dil = DIL_GROUPS[gi][1]
        x = jnp.concatenate([ref[r] for r in range(dil)], axis=0)
        if dil == 1:
            return x
        return _dot_exact_lhs(_stride_perm(rows, dil, transpose=True), x, passes=passes)

    ls = [position_order(l_ref, gi, 3) for gi, l_ref in enumerate((l0_ref, l1_ref, l2_ref))]
    mx = jnp.maximum(jnp.maximum(ls[0], ls[1]), ls[2])
    es = [jnp.exp(l - mx) for l in ls]
    inv = 1.0 / (es[0] + es[1] + es[2])
    expand = ((_iota((LANES, ATT_DIM), 1) >> 6) == _iota((LANES, ATT_DIM), 0)).astype(BF16)
    acc = None
    for gi, o_ref in enumerate((o0_ref, o1_ref, o2_ref)):
        term = _dot_exact_rhs(es[gi] * inv, expand, passes=2) * position_order(o_ref, gi, 2)
        acc = term if acc is None else acc + term
    out_ref[...] = acc.astype(out_ref.dtype)


def _merge_groups(os, lses, Bn, T, out_dtype):
    rows = MERGE_ROWS
    nblk = T // rows
    ospecs, lspecs = [], []
    for _, dil in DIL_GROUPS:
        ospecs.append(pl.BlockSpec((None, dil, rows // dil, ATT_DIM), lambda b, i: (b, 0, i, 0)))
        lspecs.append(pl.BlockSpec((None, dil, rows // dil, LANES), lambda b, i: (b, 0, i, 0)))
    return pl.pallas_call(
        _merge_body,
        out_shape=jax.ShapeDtypeStruct((Bn * T, ATT_DIM), out_dtype),
        grid=(Bn, nblk),
        in_specs=ospecs + lspecs,
        out_specs=pl.BlockSpec((rows, ATT_DIM), lambda b, i: (b * nblk + i, 0)),
        compiler_params=_cparams(("parallel", "parallel")),
        name="attn_merge",
    )(*os, *lses)


ATT_STEP_CACHE_BLOCK_BYTES = 16 * 1024 * 1024


def _attn_step_body(q_ref, k_ref, v_ref, c_ref, o_ref, lse_ref, *, rows, dil):
    scale = ATT_DH ** -0.5
    L = c_ref.shape[-1]
    valid = (_iota((8, L), 1) & (dil - 1)) == 0
    r8 = _iota((8, ATT_DH), 0)
    r8l = _iota((8, L), 0)
    items = [(bb, hg) for bb in range(rows) for hg in range(ATT_H // 8)]
    hsl = lambda hg: slice(8 * hg, 8 * hg + 8)
    q8s = [q_ref[bb, hsl(hg), :] for bb, hg in items]
    s8s = []
    for (bb, hg), q8 in zip(items, q8s):
        s8 = jnp.zeros((8, L), F32)
        for hh in range(8):
            s8 = jnp.where(r8l == hh, _dot(q8, c_ref[bb, 0, 8 * hg + hh]), s8)
        s8s.append(jnp.where(valid, s8 * scale, NEG_INF))
    ps, pns, ls, ms = [], [], [], []
    for (bb, hg), q8, s8 in zip(items, q8s, s8s):
        sn = jnp.sum(q8 * k_ref[bb, hsl(hg), :], axis=-1, keepdims=True) * scale
        m = jnp.maximum(jnp.max(s8, axis=-1, keepdims=True), sn)
        p = jnp.exp(s8 - m)
        pn = jnp.exp(sn - m)
        ps.append(p)
        pns.append(pn)
        ms.append(m)
        ls.append(jnp.sum(p, axis=-1, keepdims=True) + pn)
    for (bb, hg), p, pn, l, m in zip(items, ps, pns, ls, ms):
        o8 = jnp.zeros((8, ATT_DH), F32)
        for hh in range(8):
            o8 = jnp.where(r8 == hh, _dot_nt(p, c_ref[bb, 1, 8 * hg + hh]), o8)
        o_ref[bb, hsl(hg), :] = (o8 + pn * v_ref[bb, hsl(hg), :]) * (1.0 / l)
        lse_ref[bb, hsl(hg), :] = jnp.broadcast_to(m + jnp.log(l), (8, ATT_DH))


def _attn_step(q3, k3, v3, cache, layer, gi):
    Bs = q3.shape[0]
    window, dil = DIL_GROUPS[gi]
    L = cache.shape[2]
    assert L == window and L % dil == 0 and dil & (dil - 1) == 0
    ct = jnp.transpose(cache, (0, 1, 3, 4, 5, 2))
    R = max(1, min(8, ATT_STEP_CACHE_BLOCK_BYTES // (2 * ATT_DIM * L * 4)))
    assert Bs % R == 0
    rspec = pl.BlockSpec((R, ATT_H, ATT_DH), lambda i: (i, 0, 0))
    return pl.pallas_call(
        functools.partial(_attn_step_body, rows=R, dil=dil),
        out_shape=(jax.ShapeDtypeStruct((Bs, ATT_H, ATT_DH), F32),
                   jax.ShapeDtypeStruct((Bs, ATT_H, ATT_DH), F32)),
        grid=(Bs // R,),
        in_specs=[rspec, rspec, rspec,
                  pl.BlockSpec((None, R, 2, ATT_H, ATT_DH, L), lambda i: (layer, i, 0, 0, 0, 0))],
        out_specs=(rspec, rspec),
        compiler_params=_cparams(("parallel",)),
        name="attn_step",
    )(q3, k3, v3, ct)


def _merge_step_body(o0_ref, o1_ref, o2_ref, l0_ref, l1_ref, l2_ref, out_ref):
    l0, l1, l2 = l0_ref[...], l1_ref[...], l2_ref[...]
    mx = jnp.maximum(jnp.maximum(l0, l1), l2)
    e0, e1, e2 = jnp.exp(l0 - mx), jnp.exp(l1 - mx), jnp.exp(l2 - mx)
    inv = 1.0 / (e0 + e1 + e2)
    out_ref[...] = (e0 * inv) * o0_ref[...] + (e1 * inv) * o1_ref[...] + (e2 * inv) * o2_ref[...]


def _merge_step(os, lses):
    shape = os[0].shape
    spec = pl.BlockSpec(shape, lambda i: (0, 0, 0))
    return pl.pallas_call(
        _merge_step_body,
        out_shape=jax.ShapeDtypeStruct(shape, F32),
        grid=(1,),
        in_specs=[spec] * 6,
        out_specs=spec,
        compiler_params=_cparams(("arbitrary",)),
        name="attn_merge_step",
    )(*os, *lses)


def _router_body(h_ref, w_ref, b_ref, o_ref):
    logits = _dot3(h_ref[...].astype(F32), w_ref[...]) + b_ref[...]
    lane = _iota(logits.shape, 1)
    logits = jnp.where(lane < N_EXPERTS, logits, -jnp.inf)
    m1 = jnp.max(logits, axis=-1, keepdims=True)
    i1 = jnp.min(jnp.where(logits == m1, lane, LANES), axis=-1, keepdims=True)
    rest = jnp.where(lane == i1, -jnp.inf, logits)
    m2 = jnp.max(rest, axis=-1, keepdims=True)
    i2 = jnp.min(jnp.where(rest == m2, lane, LANES), axis=-1, keepdims=True)
    e2 = jnp.exp(m2 - m1)
    g1 = 1.0 / (1.0 + e2)
    g2 = e2 * g1
    o_ref[...] = jnp.where(lane == i1, g1, 0.0) + jnp.where(lane == i2, g2, 0.0)


def _router(h, w_router, b_router):
    M, D = h.shape
    tm = min(M, 1024)
    wp = jnp.pad(w_router, ((0, 0), (0, LANES - N_EXPERTS)))
    bp = jnp.pad(b_router, (0, LANES - N_EXPERTS)).reshape(1, LANES)
    return pl.pallas_call(
        _router_body,
        out_shape=jax.ShapeDtypeStruct((M, LANES), F32),
        grid=(M // tm,),
        in_specs=[pl.BlockSpec((tm, D), lambda i: (i, 0)),
                  pl.BlockSpec((D, LANES), lambda i: (0, 0)),
                  pl.BlockSpec((1, LANES), lambda i: (0, 0))],
        out_specs=pl.BlockSpec((tm, LANES), lambda i: (i, 0)),
        compiler_params=_cparams(("parallel",)),
        name="router",
    )(h, wp, bp)


MOE_BLOCK = 512
MOE_TILE = 512
MOE_NOT_ROUTED = -float(1 << 20)


def _router_t_body(h_ref, wt_ref, b_ref, comb_ref, rank_ref, cnt_ref):
    wh, wl = _split2(wt_ref[...])
    hb = h_ref[...]
    logits = _dot_nt(wh, hb) + _dot_nt(wl, hb) + b_ref[...]
    row = _iota(logits.shape, 0)
    m1 = jnp.max(logits, axis=0, keepdims=True)
    i1 = jnp.min(jnp.where(logits == m1, row, N_EXPERTS), axis=0, keepdims=True)
    rest = jnp.where(row == i1, -jnp.inf, logits)
    m2 = jnp.max(rest, axis=0, keepdims=True)
    i2 = jnp.min(jnp.where(rest == m2, row, N_EXPERTS), axis=0, keepdims=True)
    e2 = jnp.exp(m2 - m1)
    g1 = 1.0 / (1.0 + e2)
    g2 = e2 * g1
    routed = (row == i1) | (row == i2)
    comb_ref[...] = jnp.where(row == i1, g1, 0.0) + jnp.where(row == i2, g2, 0.0)
    onehot = jnp.where(routed, 1.0, 0.0)
    tb = onehot.shape[1]
    before = (_iota((tb, tb), 0) < _iota((tb, tb), 1)).astype(BF16)
    rank = _dot(onehot, before)
    rank_ref[...] = jnp.where(routed, rank, MOE_NOT_ROUTED)
    cnt_ref[...] = jnp.broadcast_to(jnp.sum(onehot, axis=1, keepdims=True), cnt_ref.shape)


def _router_t(h, w_router, b_router):
    M, D = h.shape
    tb = MOE_BLOCK
    nblk = M // tb
    return pl.pallas_call(
        _router_t_body,
        out_shape=(jax.ShapeDtypeStruct((N_EXPERTS, M), F32), jax.ShapeDtypeStruct((N_EXPERTS, M), F32),
                   jax.ShapeDtypeStruct((nblk, N_EXPERTS, LANES), F32)),
        grid=(nblk,),
        in_specs=[pl.BlockSpec((tb, D), lambda i: (i, 0)),
                  pl.BlockSpec((N_EXPERTS, D), lambda i: (0, 0)),
                  pl.BlockSpec((N_EXPERTS, 1), lambda i: (0, 0))],
        out_specs=(pl.BlockSpec((N_EXPERTS, tb), lambda i: (0, i)),
                   pl.BlockSpec((N_EXPERTS, tb), lambda i: (0, i)),
                   pl.BlockSpec((None, N_EXPERTS, LANES), lambda i: (i, 0, 0))),
        compiler_params=_cparams(("parallel",)),
        name="router_t",
    )(h, jnp.transpose(w_router), b_router.reshape(N_EXPERTS, 1))


def _moe_work_lists(cnt, n_tiles):
    ts = MOE_TILE
    n_e, nblk = cnt.shape
    base = jnp.cumsum(cnt, axis=1) - cnt
    ntile = (jnp.sum(cnt, axis=1) + ts - 1) // ts
    tend = jnp.cumsum(ntile)
    tstart = tend - ntile
    q_lo = base // ts
    q_hi = (base + jnp.maximum(cnt, 1) - 1) // ts
    q = jnp.stack([q_lo, q_hi], axis=-1)
    valid = jnp.stack([cnt > 0, (cnt > 0) & (q_hi > q_lo)], axis=-1)
    tile = tstart[:, None, None] + q
    delta = q * ts - base[:, :, None]
    e_idx = jnp.broadcast_to(jnp.arange(n_e, dtype=jnp.int32)[:, None, None], q.shape)
    j_idx = jnp.broadcast_to(jnp.arange(nblk, dtype=jnp.int32)[None, :, None], q.shape)

    def ordered(perm, key):
        items = [jnp.transpose(a, perm).reshape(-1).astype(jnp.int32) for a in (tile, j_idx, e_idx, delta, valid)]
        order = jnp.argsort(jnp.logical_not(items[4] > 0), stable=True)
        n_valid = jnp.sum(items[4])
        pos = jnp.minimum(jnp.arange(order.shape[0]), jnp.maximum(n_valid - 1, 0))
        tile_s, j_s, e_s, d_s, _ = [a[order][pos] for a in items]
        live = (jnp.arange(order.shape[0]) < n_valid).astype(jnp.int32)
        k = tile_s if key == "tile" else j_s
        first = jnp.concatenate([jnp.ones((1,), jnp.int32), (k[1:] != k[:-1]).astype(jnp.int32)])
        return tile_s, j_s, e_s, d_s, first + 2 * live

    gather_list = ordered((0, 1, 2), "tile")
    combine_list = ordered((1, 0, 2), "blk")
    tile_expert = jnp.minimum(jnp.sum(jnp.arange(n_tiles)[:, None] >= tend[None, :], axis=1),
                              n_e - 1).astype(jnp.int32)
    return gather_list, combine_list, tile_expert


def _slot_selector(rank_ref, e, delta):
    tgt = rank_ref[pl.ds(e, 1), :] - delta.astype(F32)
    r = _iota((MOE_TILE, tgt.shape[1]), 0).astype(F32)
    return jnp.where(r == tgt, 1.0, 0.0).astype(BF16)


def _moe_gather_body(tile_ref, blk_ref, e_ref, dlt_ref, flg_ref, h_ref, rank_ref, zero_ref, o_ref, acc_scr):
    s = pl.program_id(0)
    flags = flg_ref[s]

    @pl.when((flags & 1) == 1)
    def _():
        acc_scr[...] = jnp.zeros_like(acc_scr)

    @pl.when((flags & 2) == 2)
    def _():
        sel = _slot_selector(rank_ref, e_ref[s], dlt_ref[s])
        acc_scr[...] += jnp.dot(sel, h_ref[...], preferred_element_type=F32)

    o_ref[...] = acc_scr[...].astype(o_ref.dtype)


def _moe_gather(h, rank_t, work, n_tiles):
    M, D = h.shape
    tb, ts = MOE_BLOCK, MOE_TILE
    n_items = work[0].shape[0]
    zeros = jnp.zeros((n_tiles * ts, D), h.dtype)
    return pl.pallas_call(
        _moe_gather_body,
        out_shape=jax.ShapeDtypeStruct((n_tiles * ts, D), h.dtype),
        grid_spec=pltpu.PrefetchScalarGridSpec(
            num_scalar_prefetch=5, grid=(n_items,),
            in_specs=[pl.BlockSpec((tb, D), lambda s, t, b, e, d, f: (b[s], 0)),
                      pl.BlockSpec((N_EXPERTS, tb), lambda s, t, b, e, d, f: (0, b[s])),
                      pl.BlockSpec(memory_space=pl.ANY)],
            out_specs=pl.BlockSpec((ts, D), lambda s, t, b, e, d, f: (t[s], 0)),
            scratch_shapes=[pltpu.VMEM((ts, D), F32)]),
        input_output_aliases={7: 0},
        compiler_params=_cparams(("arbitrary",)),
        name="moe_gather",
    )(*work, h, rank_t, zeros)


def _moe_combine_body(tile_ref, blk_ref, e_ref, dlt_ref, flg_ref, ys_ref, rank_ref, comb_ref, x_ref, gate_ref,
                      o_ref, acc_scr):
    s = pl.program_id(0)
    flags = flg_ref[s]

    @pl.when((flags & 1) == 1)
    def _():
        acc_scr[...] = jnp.zeros_like(acc_scr)

    @pl.when((flags & 2) == 2)
    def _():
        e = e_ref[s]
        sel = _slot_selector(rank_ref, e, dlt_ref[s])
        c3 = jnp.concatenate(_split3(comb_ref[pl.ds(e, 1), :]) + (jnp.zeros((5, sel.shape[1]), BF16),), axis=0)
        g = lax.dot_general(sel, c3, (((1,), (1,)), ((), ())), preferred_element_type=F32)
        gate_col = g[:, 0:1] + g[:, 1:2] + g[:, 2:3]
        hi, lo = _split2(ys_ref[...] * gate_col)
        acc_scr[...] += _dot_tn(sel, hi) + _dot_tn(sel, lo)

    o_ref[...] = x_ref[...] + gate_ref[...] * acc_scr[...]


def _moe_combine(ys, rank_t, comb_t, x, gate, work, rows_per_group):
    M, D = x.shape
    tb, ts = MOE_BLOCK, MOE_TILE
    n_items = work[0].shape[0]
    assert rows_per_group % tb == 0
    espec = pl.BlockSpec((N_EXPERTS, tb), lambda s, t, b, e, d, f: (0, b[s]))
    xspec = pl.BlockSpec((tb, D), lambda s, t, b, e, d, f: (b[s], 0))
    return pl.pallas_call(
        _moe_combine_body,
        out_shape=jax.ShapeDtypeStruct((M, D), F32),
        grid_spec=pltpu.PrefetchScalarGridSpec(
            num_scalar_prefetch=5, grid=(n_items,),
            in_specs=[pl.BlockSpec((ts, D), lambda s, t, b, e, d, f: (t[s], 0)),
                      espec, espec, xspec,
                      pl.BlockSpec((None, 1, D), lambda s, t, b, e, d, f: ((b[s] * tb) // rows_per_group, 0, 0))],
            out_specs=xspec,
            scratch_shapes=[pltpu.VMEM((tb, D), F32)]),
        compiler_params=_cparams(("arbitrary",)),
        name="moe_combine",
    )(*work, ys, rank_t, comb_t, x, gate.reshape(gate.shape[0], 1, D))


def _moe_top2(h, x, gate, w_router, b_router, w_gate, w_up, w_down, layer, rpg):
    M, D = x.shape
    n_tiles = 2 * M // MOE_TILE + N_EXPERTS
    comb_t, rank_t, cnt = _router_t(h, w_router, b_router)
    gather_list, combine_list, tile_expert = _moe_work_lists(
        jnp.transpose(cnt[:, :, 0]).astype(jnp.int32), n_tiles)
    xs = _moe_gather(h, rank_t, gather_list, n_tiles)
    act = _mm_grouped(xs, w_gate, layer, tile_expert, MOE_TILE, out_dtype=BF16, w2=w_up, name="moe_up")
    ys = _mm_grouped(act, w_down, layer, tile_expert, MOE_TILE, out_dtype=F32, name="moe_down")
    return _moe_combine(ys, rank_t, comb_t, x, gate, combine_list, rpg)


def _moe(h, x, gate, comb, w_gate, w_up, w_down, layer, rpg):
    d_ff = w_gate.shape[-1]
    D = x.shape[1]
    acc = None
    for e in range(N_EXPERTS):
        act = _mm(h, w_gate, (layer, e), d_ff, out_dtype=BF16, w2=w_up, name="moe_up")
        last = e == N_EXPERTS - 1
        acc = _mm(act, w_down, (layer, e), D, out_dtype=F32, rowscale=comb, rowscale_col=e, res=acc,
                  gate=gate if last else None, res2=x if last else None, rows_per_group=rpg,
                  name="moe_down")
    return acc


def _trunk(x, mod, rpg, Bn, T, is_prompt, state_gdn, state_conv_t, caches, invf, p):
    D = x.shape[1]
    d_ff = p["w_gate_d"].shape[-1]
    new_s, new_conv = [], []
    new_kv = [[] for _ in range(N_GROUPS)]
    for i in range(DEPTH):
        j = i // 2
        sh_m, sc_m, gt_m, sh_f, sc_f, gt_f = [mod[i][:, n * D:(n + 1) * D] for n in range(6)]
        h = _norm_mod(x, p["norm_mix"][i], sh_m, sc_m, rpg, BF16 if is_prompt else F32)
        if i % 2 == 0:
            proj = _mm(h, p["w_in_a"], (j,), GDN_QKV_DIM + GDN_VAL_DIM, out_dtype=F32, name="gdn_in")
            ab = _mm_small(h, p["w_in_a"][j][:, GDN_QKV_DIM + GDN_VAL_DIM:], "gdn_in_ab")
            if is_prompt:
                prep = _gdn_prep(proj, ab, p["conv_w_a"], p["a_log"], p["dt_bias"], j, Bn, T)
                og, S = _gdn_scan(prep, proj, p["norm_o_a"], j, Bn, T)
                tail = proj.reshape(Bn, T, -1)[:, T - (CONV_W - 1):, :GDN_QKV_DIM]
                new_conv.append(tail)
            else:
                og, cb_t, S = _gdn_step(proj, ab, state_conv_t, p["conv_w_a"], p["a_log"], p["dt_bias"],
                                        p["norm_o_a"], state_gdn, j)
                new_conv.append(jnp.transpose(cb_t, (1, 0, 2)))
            new_s.append(S)
            x = _mm(og, p["w_out_a"], (j,), D, out_dtype=F32, gate=gt_m, res2=x, rows_per_group=rpg,
                    name="gdn_out")
        else:
            proj = _mm(h, p["w_in_b"], (j,), 3 * N_GROUPS * ATT_DIM, out_dtype=F32, name="att_in")
            if is_prompt:
                qkvs = _rope_split(proj, invf, Bn, T)
                os, lses = [], []
                for gi, (window, dil) in enumerate(DIL_GROUPS):
                    o_g, lse_g = _attn_prompt(qkvs[gi], gi, Bn, T)
                    os.append(o_g)
                    lses.append(lse_g)
                    wn = min(window, T)
                    k_tail = _rope(proj, invf, Bn, T, T - wn, wn, 3 * gi + 1, 1, F32)
                    v_tail = proj.reshape(Bn, T, -1)[:, T - wn:, (3 * gi + 2) * ATT_DIM:(3 * gi + 3) * ATT_DIM]
                    new_kv[gi].append(jnp.stack(
                        [k_tail.reshape(Bn, wn, ATT_H, ATT_DH), v_tail.reshape(Bn, wn, ATT_H, ATT_DH)], axis=2))
                om = _merge_groups(os, lses, Bn, T, BF16)
            else:
                Bs = x.shape[0]
                pr = _rope(proj, invf, 1, Bs, 0, Bs, 0, 3 * N_GROUPS, F32, pos0=PAST_LEN, pos_step=0)
                os, lses = [], []
                for gi in range(N_GROUPS):
                    q3, k3, v3 = [pr[:, (3 * gi + n) * ATT_DIM:(3 * gi + n + 1) * ATT_DIM]
                                  .reshape(Bs, ATT_H, ATT_DH) for n in range(3)]
                    o_g, lse_g = _attn_step(q3, k3, v3, caches[gi], j, gi)
                    os.append(o_g)
                    lses.append(lse_g)
                    new_kv[gi].append(jnp.stack([k3, v3], axis=1).reshape(Bs, 1, 2, ATT_H, ATT_DH))
                om = _merge_step(os, lses).reshape(Bs, ATT_DIM)
            x = _mm(om, p["w_out_b"], (j,), D, out_dtype=F32, gate=gt_m, res2=x, rows_per_group=rpg,
                    name="att_out")
        h = _norm_mod(x, p["norm_ffn"][i], sh_f, sc_f, rpg, BF16 if is_prompt else F32)
        if i % 2 == 0:
            act = _mm(h, p["w_gate_d"], (j,), d_ff, out_dtype=BF16, w2=p["w_up_d"], name="ffn_up")
            x = _mm(act, p["w_down_d"], (j,), D, out_dtype=F32, gate=gt_f, res2=x, rows_per_group=rpg,
                    name="ffn_down")
        elif is_prompt:
            x = _moe_top2(h, x, gt_f, p["w_router"][j], p["b_router"][j],
                          p["w_gate_e"], p["w_up_e"], p["w_down_e"], j, rpg)
        else:
            comb = _router(h, p["w_router"][j], p["b_router"][j])
            x = _moe(h, x, gt_f, comb, p["w_gate_e"], p["w_up_e"], p["w_down_e"], j, rpg)
    y = _norm_mod(x, p["norm_final"], None, None, rpg, F32)
    return (y, jnp.stack(new_s), jnp.stack(new_conv),
            jnp.stack(new_kv[0]), jnp.stack(new_kv[1]), jnp.stack(new_kv[2]))


def kernel(x_prompt, x_sample, state_gdn, state_conv, cache_kv_w128, cache_kv_w512, cache_kv_w2048,
           c_prompt, c_sample, w_ada, b_ada, norm_mix, norm_ffn, norm_final,
           w_in_a, conv_w_a, a_log, dt_bias, norm_o_a, w_out_a, w_in_b, w_out_b,
           w_gate_d, w_up_d, w_down_d, w_router, b_router, w_gate_e, w_up_e, w_down_e):
    Bp, T, D = x_prompt.shape
    Bs, Ts, _ = x_sample.shape
    assert Ts == 1 and cache_kv_w128.shape[2] == DIL_GROUPS[0][0]
    p = dict(norm_mix=norm_mix, norm_ffn=norm_ffn, norm_final=norm_final, w_in_a=w_in_a, conv_w_a=conv_w_a,
             a_log=a_log, dt_bias=dt_bias, norm_o_a=norm_o_a, w_out_a=w_out_a, w_in_b=w_in_b, w_out_b=w_out_b,
             w_gate_d=w_gate_d, w_up_d=w_up_d, w_down_d=w_down_d, w_router=w_router, b_router=b_router,
             w_gate_e=w_gate_e, w_up_e=w_up_e, w_down_e=w_down_e)

    n_c = Bp + Bs
    pad = (-n_c) % 8
    c_all = jnp.concatenate([c_prompt, c_sample, jnp.zeros((pad, D), F32)], axis=0)
    mods = [_mm(c_all, w_ada, (l,), 6 * D, out_dtype=F32, pre_silu=True, res=b_ada[l].reshape(1, 6 * D),
                rows_per_group=n_c + pad, name="adaln") for l in range(DEPTH)]
    mod_p = jnp.stack([m[:Bp] for m in mods])
    mod_s = jnp.stack([m[Bp:n_c] for m in mods])

    half = ATT_DH // 2
    inv_freq = ROPE_THETA ** (-jnp.arange(half, dtype=F32) / half)
    invf = jnp.tile(inv_freq, LANES // half).reshape(1, LANES)

    y_p, s_p, cv_p, kv0_p, kv1_p, kv2_p = _trunk(
        x_prompt.reshape(Bp * T, D), mod_p, T, Bp, T, True, None, None, None, invf, p)
    y_s, s_s, cv_s, kv0_s, kv1_s, kv2_s = _trunk(
        x_sample.reshape(Bs, D), mod_s, 1, Bs, 1, False, state_gdn,
        jnp.transpose(state_conv, (0, 2, 1, 3)), (cache_kv_w128, cache_kv_w512, cache_kv_w2048), invf, p)
    return (y_p.reshape(Bp, T, D), y_s.reshape(Bs, Ts, D), s_p, cv_p, kv0_p, kv1_p, kv2_p,
            s_s, cv_s, kv0_s, kv1_s, kv2_s)
```

```python
import functools
import math

import jax
import jax.numpy as jnp
from jax import lax
from jax.experimental import pallas as pl
from jax.experimental.pallas import tpu as pltpu

F32 = jnp.float32
BF16 = jnp.bfloat16

DEPTH = 4
GDN_HK = 8
GDN_HV = 16
GDN_DK = 128
GDN_DV = 128
GDN_KEY_DIM = GDN_HK * GDN_DK
GDN_VAL_DIM = GDN_HV * GDN_DV
GDN_QKV_DIM = 2 * GDN_KEY_DIM + GDN_VAL_DIM
CONV_W = 4
GDN_CHUNK = 64
ATT_H = 16
ATT_DH = 64
ATT_DIM = ATT_H * ATT_DH
DIL_GROUPS = ((128, 1), (512, 4), (2048, 16))
N_GROUPS = len(DIL_GROUPS)
ATT_BLOCK = 128
ROPE_THETA = 10000.0
N_EXPERTS = 8
EPS = 1e-6
NEG_INF = -1e30
PAST_LEN = 2048

V7X_VMEM_BYTES = 64 * 1024 * 1024
VMEM_LIMIT_BYTES = 52 * 1024 * 1024
VMEM_BLOCK_BUDGET = 40 * 1024 * 1024
LANES = 128


def _cparams(sem):
    return pltpu.CompilerParams(dimension_semantics=sem, vmem_limit_bytes=VMEM_LIMIT_BYTES)


def _dot(a, b):
    return jnp.dot(a.astype(BF16), b.astype(BF16), preferred_element_type=F32)


def _dot_nt(a, b):
    return lax.dot_general(a.astype(BF16), b.astype(BF16), (((1,), (1,)), ((), ())),
                           preferred_element_type=F32)


def _dot_tn(a, b):
    return lax.dot_general(a.astype(BF16), b.astype(BF16), (((0,), (0,)), ((), ())),
                           preferred_element_type=F32)


def _split2(a):
    hi = a.astype(BF16)
    lo = (a - hi.astype(F32)).astype(BF16)
    return hi, lo


def _split3(a):
    hi = a.astype(BF16)
    r = a - hi.astype(F32)
    mid = r.astype(BF16)
    lo = (r - mid.astype(F32)).astype(BF16)
    return hi, mid, lo


def _dot3(a, b):
    ah, al = _split2(a)
    bh, bl = _split2(b)
    return _dot(ah, bh) + (_dot(ah, bl) + _dot(al, bh))


def _dot_exact_lhs(c, b, passes=3):
    parts = _split3(b) if passes == 3 else _split2(b)
    acc = _dot(c, parts[0])
    for p in parts[1:]:
        acc = acc + _dot(c, p)
    return acc


def _dot_exact_rhs(a, c, passes=3):
    parts = _split3(a) if passes == 3 else _split2(a)
    acc = _dot(parts[0], c)
    for p in parts[1:]:
        acc = acc + _dot(p, c)
    return acc


def _silu(x):
    return x * jax.nn.sigmoid(x)


def _iota(shape, dim):
    return lax.broadcasted_iota(jnp.int32, shape, dim)


def _norm_mod_body(x_ref, nw_ref, *rest, modulate):
    if modulate:
        sh_ref, sc_ref, o_ref = rest
    else:
        (o_ref,) = rest
    x = x_ref[...].astype(F32)
    y = x * lax.rsqrt(jnp.mean(x * x, axis=-1, keepdims=True) + EPS) * nw_ref[...]
    if modulate:
        y = y * (1.0 + sc_ref[...]) + sh_ref[...]
    o_ref[...] = y.astype(o_ref.dtype)


def _norm_mod(x, nw, shift, scale, rows_per_group, out_dtype):
    M, D = x.shape
    tm = min(M, 1024)
    assert M % tm == 0
    modulate = shift is not None
    in_specs = [pl.BlockSpec((tm, D), lambda i: (i, 0)), pl.BlockSpec((1, D), lambda i: (0, 0))]
    args = [x, nw.reshape(1, D)]
    if modulate:
        if rows_per_group == 1:
            spec = pl.BlockSpec((tm, D), lambda i: (i, 0))
            args += [shift, scale]
        else:
            assert rows_per_group % tm == 0
            spec = pl.BlockSpec((None, 1, D), lambda i: ((i * tm) // rows_per_group, 0, 0))
            args += [shift.reshape(-1, 1, D), scale.reshape(-1, 1, D)]
        in_specs += [spec, spec]
    return pl.pallas_call(
        functools.partial(_norm_mod_body, modulate=modulate),
        out_shape=jax.ShapeDtypeStruct((M, D), out_dtype),
        grid=(M // tm,),
        in_specs=in_specs,
        out_specs=pl.BlockSpec((tm, D), lambda i: (i, 0)),
        compiler_params=_cparams(("parallel",)),
        name="norm_mod",
    )(*args)


def _mm_body(*refs, pre_silu, swiglu, has_rowscale, rowscale_col, has_res, has_gate, has_res2):
    it = iter(refs)
    x_ref = next(it)
    w_ref = next(it)
    w2_ref = next(it) if swiglu else None
    rs_ref = next(it) if has_rowscale else None
    res_ref = next(it) if has_res else None
    gate_ref = next(it) if has_gate else None
    res2_ref = next(it) if has_res2 else None
    o_ref = next(it)
    x = x_ref[...]
    if pre_silu:
        x = _silu(x.astype(F32))
    xb = x.astype(BF16)
    acc = jnp.dot(xb, w_ref[...].astype(BF16), preferred_element_type=F32)
    if swiglu:
        acc = _silu(acc) * jnp.dot(xb, w2_ref[...].astype(BF16), preferred_element_type=F32)
    if has_rowscale:
        acc = acc * rs_ref[...][:, rowscale_col:rowscale_col + 1]
    if has_res:
        acc = acc + res_ref[...]
    if has_gate:
        acc = acc * gate_ref[...]
    if has_res2:
        acc = acc + res2_ref[...]
    o_ref[...] = acc.astype(o_ref.dtype)


def _pick_tiles(M, K, N, x_bytes, n_w, out_bytes, n_res):
    def est(tm_, tn_):
        blocks = tm_ * K * x_bytes + n_w * K * tn_ * 4 + tm_ * tn_ * (out_bytes + 4 * n_res)
        temps = n_w * K * tn_ * 2 + tm_ * K * 2 + (1 + n_w) * tm_ * tn_ * 4
        return 2 * blocks + temps

    tms = [t for t in (1024, 512, 256, 128) if t <= M and M % t == 0] or [M]
    tns = [t for t in (1024, 896, 768, 512, 384, 256, 128) if N % t == 0] or [N]
    wide = [(a, b) for a in tms for b in tns if a >= 512 and b >= 512]
    rest = [(a, b) for a in tms for b in tns if (a, b) not in wide]
    for tm, tn in wide + rest:
        if est(tm, tn) <= VMEM_BLOCK_BUDGET:
            return tm, tn
    return tms[-1], tns[-1]


def _mm(x, w, lead, n_cols, col0=0, *, out_dtype, w2=None, pre_silu=False, rowscale=None,
        rowscale_col=0, res=None, gate=None, res2=None, rows_per_group=1, name="mm"):
    M, K = x.shape
    swiglu = w2 is not None
    n_res = int(res is not None) + int(res2 is not None) + int(gate is not None)
    tm, tn = _pick_tiles(M, K, n_cols, x.dtype.itemsize, 2 if swiglu else 1,
                         jnp.dtype(out_dtype).itemsize, n_res)
    assert col0 % tn == 0 and n_cols % tn == 0
    nj, ni = n_cols // tn, M // tm
    jb0 = col0 // tn
    nlead = len(lead)
    wblock = (None,) * nlead + (K, tn)

    def wmap(j, i):
        return tuple(lead) + (0, jb0 + j)

    in_specs = [pl.BlockSpec((tm, K), lambda j, i: (i, 0)), pl.BlockSpec(wblock, wmap)]
    args = [x, w]
    if swiglu:
        in_specs.append(pl.BlockSpec(wblock, wmap))
        args.append(w2)
    if rowscale is not None:
        in_specs.append(pl.BlockSpec((tm, rowscale.shape[1]), lambda j, i: (i, 0)))
        args.append(rowscale)

    def tile_spec(a):
        if a.shape[0] == M:
            return pl.BlockSpec((tm, tn), lambda j, i: (i, j)), a
        assert rows_per_group % tm == 0 and a.shape[0] * rows_per_group == M
        return (pl.BlockSpec((None, 1, tn), lambda j, i: ((i * tm) // rows_per_group, 0, j)),
                a.reshape(a.shape[0], 1, a.shape[1]))

    for a in (res, gate, res2):
        if a is not None:
            s, a2 = tile_spec(a)
            in_specs.append(s)
            args.append(a2)
    body = functools.partial(
        _mm_body, pre_silu=pre_silu, swiglu=swiglu, has_rowscale=rowscale is not None,
        rowscale_col=rowscale_col, has_res=res is not None, has_gate=gate is not None,
        has_res2=res2 is not None)
    return pl.pallas_call(
        body,
        out_shape=jax.ShapeDtypeStruct((M, n_cols), out_dtype),
        grid=(nj, ni),
        in_specs=in_specs,
        out_specs=pl.BlockSpec((tm, tn), lambda j, i: (i, j)),
        compiler_params=_cparams(("parallel", "parallel")),
        name=name,
    )(*args)


def _mm_grouped(x, w, layer, tile_expert, tm, *, out_dtype, w2=None, name="mm_grouped"):
    M, K = x.shape
    N = w.shape[-1]
    swiglu = w2 is not None
    tn = next(t for t in (896, 512, 256, 128) if N % t == 0)
    nj, ni = N // tn, M // tm
    wspec = pl.BlockSpec((None, None, K, tn), lambda j, i, te: (layer, te[i], 0, j))
    in_specs = [pl.BlockSpec((tm, K), lambda j, i, te: (i, 0)), wspec]
    args = [x, w]
    if swiglu:
        in_specs.append(wspec)
        args.append(w2)
    inner = functools.partial(_mm_body, pre_silu=False, swiglu=swiglu, has_rowscale=False, rowscale_col=0,
                              has_res=False, has_gate=False, has_res2=False)

    def body(te_ref, *refs):
        inner(*refs)

    return pl.pallas_call(
        body,
        out_shape=jax.ShapeDtypeStruct((M, N), out_dtype),
        grid_spec=pltpu.PrefetchScalarGridSpec(
            num_scalar_prefetch=1, grid=(nj, ni), in_specs=in_specs,
            out_specs=pl.BlockSpec((tm, tn), lambda j, i, te: (i, j))),
        compiler_params=_cparams(("parallel", "parallel")),
        name=name,
    )(tile_expert, *args)


def _mm_small(x, w_small, name):
    M, K = x.shape
    n = w_small.shape[1]
    tm = min(M, 1024)

    def body(x_ref, w_ref, o_ref):
        o_ref[...] = _dot(x_ref[...], w_ref[...])

    return pl.pallas_call(
        body,
        out_shape=jax.ShapeDtypeStruct((M, n), F32),
        grid=(M // tm,),
        in_specs=[pl.BlockSpec((tm, K), lambda i: (i, 0)), pl.BlockSpec((K, n), lambda i: (0, 0))],
        out_specs=pl.BlockSpec((tm, n), lambda i: (i, 0)),
        compiler_params=_cparams(("parallel",)),
        name=name,
    )(x, w_small)


GDN_PREP_ROWS = 512
GDN_PREP_CHUNKS = GDN_PREP_ROWS // GDN_CHUNK
PAIR = 2 * GDN_CHUNK


def _gdn_gates(ab, alog, dtb):
    a = ab[:, :GDN_HV]
    b = ab[:, GDN_HV:]
    xg = a + dtb
    softplus = jnp.maximum(xg, 0.0) + jnp.log1p(jnp.exp(-jnp.abs(xg)))
    return -jnp.exp(alog) * softplus, jax.nn.sigmoid(b)


def _gdn_prep_body(q_ref, qp_ref, k_ref, kp_ref, v_ref, vp_ref, cwq_ref, cwk_ref, cwv_ref,
                   ab_ref, alog_ref, dtb_ref,
                   u_ref, w_ref, qd_ref, kd_ref, qk_ref, gl_ref):
    i = pl.program_id(1)
    hk = pl.program_id(2)
    rows = GDN_PREP_ROWS
    first = i == 0

    def conv(x_ref, xp_ref, cw_ref):
        cur = x_ref[...]
        prev = jnp.where(first, 0.0, xp_ref[...])
        ext = jnp.concatenate([prev, cur], axis=0)
        cw = cw_ref[...]
        y = cur * cw[CONV_W - 1:CONV_W, :]
        for s in range(1, CONV_W):
            y = y + ext[8 - s:8 - s + rows, :] * cw[CONV_W - 1 - s:CONV_W - s, :]
        return _silu(y)

    qc = conv(q_ref, qp_ref, cwq_ref)
    kc = conv(k_ref, kp_ref, cwk_ref)
    vc = conv(v_ref, vp_ref, cwv_ref)
    qn = qc * lax.rsqrt(jnp.sum(qc * qc, axis=-1, keepdims=True) + EPS) * (GDN_DK ** -0.5)
    kn = kc * lax.rsqrt(jnp.sum(kc * kc, axis=-1, keepdims=True) + EPS)

    g16, be16 = _gdn_gates(ab_ref[...], alog_ref[...], dtb_ref[...])
    r16 = _iota((GDN_HV, LANES), 0)

    def head_bcast(x16, hv):
        return _dot_exact_rhs(x16, (r16 == hv).astype(BF16))

    g_b = [head_bcast(g16, 2 * hk + s) for s in (0, 1)]
    be_b = [head_bcast(be16, 2 * hk + s) for s in (0, 1)]

    ri = _iota((PAIR, PAIR), 0)
    ci = _iota((PAIR, PAIR), 1)
    same = (ri >> 6) == (ci >> 6)
    causal = same & (ri >= ci)
    strict = same & (ri > ci)
    blk16 = (ri >> 4) == (ci >> 4)
    tri = causal.astype(BF16)
    eye = (ri == ci).astype(F32)
    r8 = _iota((8, LANES), 0)

    chunks = range(GDN_PREP_CHUNKS)
    sls = [slice(c * GDN_CHUNK, (c + 1) * GDN_CHUNK) for c in chunks]
    k2 = [jnp.concatenate([kn[sl], kn[sl]], axis=0) for sl in sls]
    q2 = [jnp.concatenate([qn[sl], qn[sl]], axis=0) for sl in sls]
    g2 = [jnp.concatenate([g_b[0][sl], g_b[1][sl]], axis=0) for sl in sls]
    b2 = [jnp.concatenate([be_b[0][sl], be_b[1][sl]], axis=0) for sl in sls]

    res = [_dot_exact_lhs(tri, jnp.concatenate([jnp.where(strict, g, 0.0), g], axis=1), passes=2) for g in g2]
    kk = [_dot_nt(k, k) for k in k2]
    qkr = [_dot_nt(q, k) for q, k in zip(q2, k2)]
    gcum = [r[:, PAIR:] for r in res]
    decay = [jnp.where(causal, jnp.exp(jnp.where(causal, r[:, :PAIR], 0.0)), 0.0) for r in res]
    eg = [jnp.exp(g) for g in gcum]
    lmat = [jnp.where(strict, b * k * d, 0.0) for b, k, d in zip(b2, kk, decay)]

    ld = [jnp.where(blk16, l, 0.0) for l in lmat]
    nn = [jnp.where(blk16, 0.0, l) for l in lmat]
    x = [eye - l for l in ld]
    p = [_dot(l, l) for l in ld]
    for _ in range(2):
        x = [xi + _dot(xi, pi) for xi, pi in zip(x, p)]
        p = [_dot(pi, pi) for pi in p]
    x = [xi + _dot(xi, pi) for xi, pi in zip(x, p)]
    v2 = [jnp.concatenate([vc[sl, :GDN_DV], vc[sl, GDN_DV:]], axis=0) for sl in sls]
    rhs = [jnp.concatenate([b * v, b * e * k], axis=1) for b, v, e, k in zip(b2, v2, eg, k2)]
    mm = [_dot(xi, n) for xi, n in zip(x, nn)]
    y = [_dot(xi, r) for xi, r in zip(x, rhs)]
    m2 = [_dot(m, m) for m in mm]
    y = [yi + _dot(m, yi) for yi, m in zip(y, m2)]
    y = [yi - _dot(m, yi) for yi, m in zip(y, mm)]

    for c in chunks:
        gl0 = gcum[c][GDN_CHUNK - 1:GDN_CHUNK, :]
        gl1 = gcum[c][PAIR - 1:PAIR, :]
        glast = jnp.where(ri < GDN_CHUNK, gl0, gl1)
        u_ref[c] = y[c][:, :GDN_DV]
        w_ref[c] = y[c][:, GDN_DV:].astype(BF16)
        qd_ref[c] = (q2[c] * eg[c]).astype(BF16)
        kd_ref[c] = (k2[c] * jnp.exp(glast - gcum[c])).astype(BF16)
        qk_ref[c] = (qkr[c] * decay[c]).astype(BF16)
        gl_ref[c] = jnp.where(r8 == 0, jnp.exp(gl0), jnp.where(r8 == 1, jnp.exp(gl1), 0.0))


def _gdn_prep(proj, ab, conv_w, a_log, dt_bias, layer, Bn, T):
    rows = GDN_PREP_ROWS
    nblk = T // rows
    nch = T // GDN_CHUNK
    cpb = GDN_PREP_CHUNKS

    def cur(width, col):
        return pl.BlockSpec((rows, width), lambda b, i, h: (b * nblk + i, col(h)))

    def prev(width, col):
        return pl.BlockSpec((8, width),
                            lambda b, i, h: (jnp.maximum((b * T + i * rows) // 8 - 1, 0), col(h)))

    def cw(width, col):
        return pl.BlockSpec((None, CONV_W, width), lambda b, i, h: (layer, 0, col(h)))

    qcol = lambda h: h
    kcol = lambda h: GDN_HK + h
    vcol = lambda h: GDN_HK + h
    in_specs = [cur(128, qcol), prev(128, qcol), cur(128, kcol), prev(128, kcol),
                cur(256, vcol), prev(256, vcol),
                cw(128, qcol), cw(128, kcol), cw(256, vcol),
                pl.BlockSpec((rows, 2 * GDN_HV), lambda b, i, h: (b * nblk + i, 0)),
                pl.BlockSpec((None, 1, GDN_HV), lambda b, i, h: (layer, 0, 0)),
                pl.BlockSpec((None, 1, GDN_HV), lambda b, i, h: (layer, 0, 0))]
    big = lambda dt: jax.ShapeDtypeStruct((Bn, GDN_HK, nch, PAIR, LANES), dt)
    ospec = pl.BlockSpec((None, None, cpb, PAIR, LANES), lambda b, i, h: (b, h, i, 0, 0))
    out_shape = (big(F32), big(BF16), big(BF16), big(BF16), big(BF16),
                 jax.ShapeDtypeStruct((Bn, GDN_HK, nch, 8, LANES), F32))
    out_specs = (ospec, ospec, ospec, ospec, ospec,
                 pl.BlockSpec((None, None, cpb, 8, LANES), lambda b, i, h: (b, h, i, 0, 0)))
    return pl.pallas_call(
        _gdn_prep_body,
        out_shape=out_shape,
        grid=(Bn, nblk, GDN_HK),
        in_specs=in_specs,
        out_specs=out_specs,
        compiler_params=_cparams(("parallel", "parallel", "parallel")),
        name="gdn_prep",
    )(proj, proj, proj, proj, proj, proj, conv_w, conv_w, conv_w, ab,
      a_log.reshape(-1, 1, GDN_HV), dt_bias.reshape(-1, 1, GDN_HV))


GDN_SCAN_ROWS = 256
GDN_SCAN_CHUNKS = GDN_SCAN_ROWS // GDN_CHUNK


def _gated_norm_store(o_scr, z_ref, no_ref, og_ref):
    for h in range(GDN_HV):
        sl = slice(h * GDN_DV, (h + 1) * GDN_DV)
        oh = o_scr[:, sl]
        zh = z_ref[:, sl]
        on = oh * lax.rsqrt(jnp.mean(oh * oh, axis=-1, keepdims=True) + EPS) * no_ref[...]
        og_ref[:, sl] = (on * _silu(zh)).astype(og_ref.dtype)


def _gdn_scan_body(u_ref, w_ref, qd_ref, kd_ref, qk_ref, gl_ref, z_ref, no_ref,
                   og_ref, s_ref, o_scr):
    @pl.when(pl.program_id(1) == 0)
    def _():
        s_ref[...] = jnp.zeros_like(s_ref)

    C = GDN_CHUNK
    heads = range(GDN_HV)
    for c in range(GDN_SCAN_CHUNKS):
        half = lambda ref, hv: ref[hv // 2, c][(hv % 2) * C:(hv % 2 + 1) * C]
        s = [s_ref[hv] for hv in heads]
        ws = [_dot(half(w_ref, hv), s[hv]) for hv in heads]
        qs = [_dot(half(qd_ref, hv), s[hv]) for hv in heads]
        vn = [half(u_ref, hv) - ws[hv] for hv in heads]
        oi = [_dot(qk_ref[hk, c], jnp.concatenate([vn[2 * hk], vn[2 * hk + 1]], axis=0))
              for hk in range(GDN_HK)]
        kv = [_dot_tn(half(kd_ref, hv), vn[hv]) for hv in heads]
        for hv in heads:
            gl = gl_ref[hv // 2, c]
            s_ref[hv] = s[hv] * gl[hv % 2:hv % 2 + 1, :] + kv[hv]
            o_scr[c * C:(c + 1) * C, hv * GDN_DV:(hv + 1) * GDN_DV] = (
                qs[hv] + oi[hv // 2][(hv % 2) * C:(hv % 2 + 1) * C])
    _gated_norm_store(o_scr, z_ref, no_ref, og_ref)


def _gdn_scan(prep, proj, norm_o, layer, Bn, T):
    u, w, qd, kd, qk, gl = prep
    rows = GDN_SCAN_ROWS
    nblk = T // rows
    cpb = GDN_SCAN_CHUNKS
    bspec = pl.BlockSpec((None, GDN_HK, cpb, PAIR, LANES), lambda b, i: (b, 0, i, 0, 0))
    in_specs = [bspec, bspec, bspec, bspec, bspec,
                pl.BlockSpec((None, GDN_HK, cpb, 8, LANES), lambda b, i: (b, 0, i, 0, 0)),
                pl.BlockSpec((rows, GDN_VAL_DIM), lambda b, i: (b * nblk + i, GDN_QKV_DIM // GDN_VAL_DIM)),
                pl.BlockSpec((None, 1, GDN_DV), lambda b, i: (layer, 0, 0))]
    out_shape = (jax.ShapeDtypeStruct((Bn * T, GDN_VAL_DIM), BF16),
                 jax.ShapeDtypeStruct((Bn, GDN_HV, GDN_DK, GDN_DV), F32))
    out_specs = (pl.BlockSpec((rows, GDN_VAL_DIM), lambda b, i: (b * nblk + i, 0)),
                 pl.BlockSpec((None, GDN_HV, GDN_DK, GDN_DV), lambda b, i: (b, 0, 0, 0)))
    return pl.pallas_call(
        _gdn_scan_body,
        out_shape=out_shape,
        grid=(Bn, nblk),
        in_specs=in_specs,
        out_specs=out_specs,
        scratch_shapes=[pltpu.VMEM((rows, GDN_VAL_DIM), F32)],
        compiler_params=_cparams(("parallel", "arbitrary")),
        name="gdn_scan",
    )(u, w, qd, kd, qk, gl, proj, norm_o.reshape(-1, 1, GDN_DV))


GDN_STEP_ROWS = 8


def _gdn_step_body(proj_ref, ab_ref, cb_ref, cw_ref, alog_ref, dtb_ref, no_ref, st_ref,
                   og_ref, cbo_ref, sto_ref, o_scr):
    R = GDN_STEP_ROWS
    qkv = proj_ref[:, :GDN_QKV_DIM]
    cw = cw_ref[...]
    y = cb_ref[0] * cw[0:1, :]
    y = y + cb_ref[1] * cw[1:2, :]
    y = y + cb_ref[2] * cw[2:3, :]
    y = y + qkv * cw[3:4, :]
    qkv_c = _silu(y)
    cbo_ref[0] = cb_ref[1]
    cbo_ref[1] = cb_ref[2]
    cbo_ref[2] = qkv

    g16, be16 = _gdn_gates(ab_ref[...], alog_ref[...], dtb_ref[...])
    expand = ((_iota((GDN_HV, GDN_VAL_DIM), 1) >> 7) == _iota((GDN_HV, GDN_VAL_DIM), 0)).astype(BF16)
    eg_f = jnp.exp(_dot_exact_rhs(g16, expand))
    be_f = _dot_exact_rhs(be16, expand)
    r8 = _iota((8, LANES), 0)

    for hk in range(GDN_HK):
        qh = qkv_c[:, hk * GDN_DK:(hk + 1) * GDN_DK]
        kh = qkv_c[:, GDN_KEY_DIM + hk * GDN_DK:GDN_KEY_DIM + (hk + 1) * GDN_DK]
        qn = qh * lax.rsqrt(jnp.sum(qh * qh, axis=-1, keepdims=True) + EPS) * (GDN_DK ** -0.5)
        kn = kh * lax.rsqrt(jnp.sum(kh * kh, axis=-1, keepdims=True) + EPS)
        qk = jnp.sum(qn * kn, axis=-1, keepdims=True)
        for s in (0, 1):
            hv = 2 * hk + s
            sl = slice(hv * GDN_DV, (hv + 1) * GDN_DV)
            eg = eg_f[:, sl]
            be = be_f[:, sl]
            vh = qkv_c[:, 2 * GDN_KEY_DIM + hv * GDN_DV:2 * GDN_KEY_DIM + (hv + 1) * GDN_DV]
            w_rows = be * eg * kn
            qd_rows = qn * eg
            rows = range(R)
            st = [st_ref[bb, hv] for bb in rows]
            rr = [_dot(jnp.where(r8 == 0, w_rows[bb:bb + 1], jnp.where(r8 == 1, qd_rows[bb:bb + 1], 0.0)),
                       st[bb]) for bb in rows]
            vn = [be[bb:bb + 1] * vh[bb:bb + 1] - rr[bb][0:1] for bb in rows]
            kv = [_dot_tn(jnp.where(r8 == 0, kn[bb:bb + 1], 0.0), jnp.where(r8 == 0, vn[bb], 0.0))
                  for bb in rows]
            for bb in rows:
                sto_ref[bb, hv] = st[bb] * eg[bb:bb + 1] + kv[bb]
                o_scr[bb:bb + 1, sl] = rr[bb][1:2] + qk[bb:bb + 1] * vn[bb]
    _gated_norm_store(o_scr, proj_ref.at[:, GDN_QKV_DIM:], no_ref, og_ref)


def _gdn_step(proj, ab, conv_buf_t, conv_w, a_log, dt_bias, norm_o, state, layer):
    Bs = proj.shape[0]
    R = GDN_STEP_ROWS
    vec = lambda n: pl.BlockSpec((None, 1, n), lambda i: (layer, 0, 0))
    in_specs = [pl.BlockSpec((R, proj.shape[1]), lambda i: (i, 0)),
                pl.BlockSpec((R, 2 * GDN_HV), lambda i: (i, 0)),
                pl.BlockSpec((None, CONV_W - 1, R, GDN_QKV_DIM), lambda i: (layer, 0, i, 0)),
                pl.BlockSpec((None, CONV_W, GDN_QKV_DIM), lambda i: (layer, 0, 0)),
                vec(GDN_HV), vec(GDN_HV), vec(GDN_DV),
                pl.BlockSpec((None, R, GDN_HV, GDN_DK, GDN_DV), lambda i: (layer, i, 0, 0, 0))]
    out_shape = (jax.ShapeDtypeStruct((Bs, GDN_VAL_DIM), F32),
                 jax.ShapeDtypeStruct((CONV_W - 1, Bs, GDN_QKV_DIM), F32),
                 jax.ShapeDtypeStruct((Bs, GDN_HV, GDN_DK, GDN_DV), F32))
    out_specs = (pl.BlockSpec((R, GDN_VAL_DIM), lambda i: (i, 0)),
                 pl.BlockSpec((CONV_W - 1, R, GDN_QKV_DIM), lambda i: (0, i, 0)),
                 pl.BlockSpec((R, GDN_HV, GDN_DK, GDN_DV), lambda i: (i, 0, 0, 0)))
    return pl.pallas_call(
        _gdn_step_body,
        out_shape=out_shape,
        grid=(Bs // R,),
        in_specs=in_specs,
        out_specs=out_specs,
        scratch_shapes=[pltpu.VMEM((R, GDN_VAL_DIM), F32)],
        compiler_params=_cparams(("parallel",)),
        name="gdn_step",
    )(proj, ab, conv_buf_t, conv_w, a_log.reshape(-1, 1, GDN_HV), dt_bias.reshape(-1, 1, GDN_HV),
      norm_o.reshape(-1, 1, GDN_DV), state)


def _rope_tables(ang):
    lane = _iota(ang.shape, 1)
    cs = jnp.cos(ang)
    sn = jnp.sin(ang)
    sg = jnp.where((lane & (ATT_DH - 1)) < ATT_DH // 2, -sn, sn)
    reps = ATT_DIM // LANES
    return jnp.concatenate([cs] * reps, axis=1), jnp.concatenate([sg] * reps, axis=1)


def _rope_apply(x, cs, sg):
    lane = _iota(x.shape, 1)
    first_half = (lane & (ATT_DH - 1)) < ATT_DH // 2
    half = ATT_DH // 2
    partner = jnp.where(first_half, pltpu.roll(x, x.shape[1] - half, 1), pltpu.roll(x, half, 1))
    return x * cs + partner * sg


def _rope_body(invf_ref, x_ref, o_ref, *, pos0, pos_step, row0, tr, col0):
    i = pl.program_id(1)
    c = pl.program_id(2)
    is_v = lax.rem(col0 + c, 3) == 2

    @pl.when(is_v)
    def _():
        o_ref[...] = x_ref[...].astype(o_ref.dtype)

    @pl.when(jnp.logical_not(is_v))
    def _():
        pos = (pos0 + pos_step * (row0 + i * tr + _iota((tr, LANES), 0))).astype(F32)
        cs, sg = _rope_tables(pos * invf_ref[...])
        o_ref[...] = _rope_apply(x_ref[...], cs, sg).astype(o_ref.dtype)


def _rope(proj, invf, Bn, T, row0, nrows, col0, ncols, out_dtype, pos0=0, pos_step=1):
    tr = min(nrows, 256)
    assert nrows % tr == 0 and row0 % tr == 0 and T % tr == 0
    nb = nrows // tr
    body = functools.partial(_rope_body, pos0=pos0, pos_step=pos_step, row0=row0, tr=tr, col0=col0)
    return pl.pallas_call(
        body,
        out_shape=jax.ShapeDtypeStruct((Bn * nrows, ncols * ATT_DIM), out_dtype),
        grid=(Bn, nb, ncols),
        in_specs=[pl.BlockSpec((1, LANES), lambda b, i, c: (0, 0)),
                  pl.BlockSpec((tr, ATT_DIM), lambda b, i, c: (b * (T // tr) + row0 // tr + i, col0 + c))],
        out_specs=pl.BlockSpec((tr, ATT_DIM), lambda b, i, c: (b * nb + i, c)),
        compiler_params=_cparams(("parallel", "parallel", "parallel")),
        name="rope",
    )(invf, proj)


def _attn_body(q_ref, kp_ref, kc_ref, vp_ref, vc_ref, o_ref, lse_ref):
    n = pl.program_id(2)
    blk = ATT_BLOCK
    qi = _iota((blk, 2 * blk), 0)
    kj = _iota((blk, 2 * blk), 1)
    dist = qi + blk - kj
    n_win = blk
    mask = (dist >= 0) & (dist <= n_win) & ((kj >= blk) | (n > 0))
    lane = _iota((blk, LANES), 1)
    lo = lane < ATT_DH
    lse_acc = jnp.zeros((blk, LANES), F32)
    scale = ATT_DH ** -0.5
    for hp in range(ATT_H // 2):
        sl = slice(hp * LANES, (hp + 1) * LANES)
        q2 = q_ref[:, sl].astype(F32)
        k2 = jnp.concatenate([kp_ref[:, sl], kc_ref[:, sl]], axis=0)
        v2 = jnp.concatenate([vp_ref[:, sl], vc_ref[:, sl]], axis=0)
        outs = []
        for s in (0, 1):
            qm = jnp.where(lo if s == 0 else jnp.logical_not(lo), q2, 0.0)
            sc = _dot_nt(qm, k2) * scale
            sc = jnp.where(mask, sc, NEG_INF)
            m = jnp.max(sc, axis=-1, keepdims=True)
            p = jnp.exp(sc - m)
            l = jnp.sum(p, axis=-1, keepdims=True)
            outs.append(_dot(p, v2) * (1.0 / l))
            lse_acc = jnp.where(lane == 2 * hp + s, m + jnp.log(l), lse_acc)
        o_ref[:, sl] = jnp.where(lo, outs[0], outs[1])
    lse_ref[...] = lse_acc


def _attn_prompt(qkv, gi, Bn, T):
    dil = DIL_GROUPS[gi][1]
    Td = T // dil
    nb = Td // ATT_BLOCK

    def spec(which, prev):
        def imap(b, r, n):
            return (b, r, jnp.maximum(n - 1, 0) if prev else n, which)
        return pl.BlockSpec((None, None, ATT_BLOCK, ATT_DIM), imap)

    return pl.pallas_call(
        _attn_body,
        out_shape=(jax.ShapeDtypeStruct((Bn, dil, Td, ATT_DIM), F32),
                   jax.ShapeDtypeStruct((Bn, dil, Td, LANES), F32)),
        grid=(Bn, dil, nb),
        in_specs=[spec(0, False), spec(1, True), spec(1, False), spec(2, True), spec(2, False)],
        out_specs=(pl.BlockSpec((None, None, ATT_BLOCK, ATT_DIM), lambda b, r, n: (b, r, n, 0)),
                   pl.BlockSpec((None, None, ATT_BLOCK, LANES), lambda b, r, n: (b, r, n, 0))),
        compiler_params=_cparams(("parallel", "parallel", "parallel")),
        name="attn_prompt",
    )(qkv, qkv, qkv, qkv, qkv)


ROPE_SPLIT_ROWS = 256


def _stride_perm(rows, dil, transpose=False):
    n = rows // dil
    i = _iota((rows, rows), 1 if transpose else 0)
    j = _iota((rows, rows), 0 if transpose else 1)
    shift = n.bit_length() - 1
    assert n == 1 << shift
    return (j == (i & (n - 1)) * dil + (i >> shift)).astype(BF16)


def _rope_split_body(invf_ref, x_ref, o0_ref, o1_ref, o2_ref):
    rows = ROPE_SPLIT_ROWS
    pos = (pl.program_id(1) * rows + _iota((rows, LANES), 0)).astype(F32)
    cs, sg = _rope_tables(pos * invf_ref[...])
    for gi, o_ref in enumerate((o0_ref, o1_ref, o2_ref)):
        dil = DIL_GROUPS[gi][1]
        n = rows // dil
        perm = _stride_perm(rows, dil) if dil > 1 else None
        for which in range(3):
            xx = x_ref[:, (3 * gi + which) * ATT_DIM:(3 * gi + which + 1) * ATT_DIM]
            if which < 2:
                xx = _rope_apply(xx, cs, sg)
            xb = xx.astype(BF16)
            if perm is not None:
                xb = jnp.dot(perm, xb, preferred_element_type=F32).astype(BF16)
            for r in range(dil):
                o_ref[r, :, which * ATT_DIM:(which + 1) * ATT_DIM] = xb[r * n:(r + 1) * n, :]


def _rope_split(proj, invf, Bn, T):
    rows = ROPE_SPLIT_ROWS
    nblk = T // rows
    out_shape, out_specs = [], []
    for _, dil in DIL_GROUPS:
        out_shape.append(jax.ShapeDtypeStruct((Bn, dil, T // dil, 3 * ATT_DIM), BF16))
        out_specs.append(pl.BlockSpec((None, dil, rows // dil, 3 * ATT_DIM), lambda b, i: (b, 0, i, 0)))
    return pl.pallas_call(
        _rope_split_body,
        out_shape=tuple(out_shape),
        grid=(Bn, nblk),
        in_specs=[pl.BlockSpec((1, LANES), lambda b, i: (0, 0)),
                  pl.BlockSpec((rows, 3 * N_GROUPS * ATT_DIM), lambda b, i: (b * nblk + i, 0))],
        out_specs=tuple(out_specs),
        compiler_params=_cparams(("parallel", "parallel")),
        name="rope_split",
    )(invf, proj)


MERGE_ROWS = 256


def _merge_body(o0_ref, o1_ref, o2_ref, l0_ref, l1_ref, l2_ref, out_ref):
    rows = MERGE_ROWS

    def position_order(ref, gi, passes):
        dil = DIL_GROUPS[gi][1]
        x = jnp.concatenate([ref[r] for r in range(dil)], axis=0)
        if dil == 1:
            return x
        return _dot_exact_lhs(_stride_perm(rows, dil, transpose=True), x, passes=passes)

    ls = [position_order(l_ref, gi, 3) for gi, l_ref in enumerate((l0_ref, l1_ref, l2_ref))]
    mx = jnp.maximum(jnp.maximum(ls[0], ls[1]), ls[2])
    es = [jnp.exp(l - mx) for l in ls]
    inv = 1.0 / (es[0] + es[1] + es[2])
    expand = ((_iota((LANES, ATT_DIM), 1) >> 6) == _iota((LANES, ATT_DIM), 0)).astype(BF16)
    acc = None
    for gi, o_ref in enumerate((o0_ref, o1_ref, o2_ref)):
        term = _dot_exact_rhs(es[gi] * inv, expand, passes=2) * position_order(o_ref, gi, 2)
        acc = term if acc is None else acc + term
    out_ref[...] = acc.astype(out_ref.dtype)


def _merge_groups(os, lses, Bn, T, out_dtype):
    rows = MERGE_ROWS
    nblk = T // rows
    ospecs, lspecs = [], []
    for _, dil in DIL_GROUPS:
        ospecs.append(pl.BlockSpec((None, dil, rows // dil, ATT_DIM), lambda b, i: (b, 0, i, 0)))
        lspecs.append(pl.BlockSpec((None, dil, rows // dil, LANES), lambda b, i: (b, 0, i, 0)))
    return pl.pallas_call(
        _merge_body,
        out_shape=jax.ShapeDtypeStruct((Bn * T, ATT_DIM), out_dtype),
        grid=(Bn, nblk),
        in_specs=ospecs + lspecs,
        out_specs=pl.BlockSpec((rows, ATT_DIM), lambda b, i: (b * nblk + i, 0)),
        compiler_params=_cparams(("parallel", "parallel")),
        name="attn_merge",
    )(*os, *lses)


ATT_STEP_CACHE_BLOCK_BYTES = 16 * 1024 * 1024


def _attn_step_body(q_ref, k_ref, v_ref, c_ref, o_ref, lse_ref, *, rows, dil):
    scale = ATT_DH ** -0.5
    L = c_ref.shape[-1]
    valid = (_iota((8, L), 1) & (dil - 1)) == 0
    r8 = _iota((8, ATT_DH), 0)
    r8l = _iota((8, L), 0)
    items = [(bb, hg) for bb in range(rows) for hg in range(ATT_H // 8)]
    hsl = lambda hg: slice(8 * hg, 8 * hg + 8)
    q8s = [q_ref[bb, hsl(hg), :] for bb, hg in items]
    s8s = []
    for (bb, hg), q8 in zip(items, q8s):
        s8 = jnp.zeros((8, L), F32)
        for hh in range(8):
            s8 = jnp.where(r8l == hh, _dot(q8, c_ref[bb, 0, 8 * hg + hh]), s8)
        s8s.append(jnp.where(valid, s8 * scale, NEG_INF))
    ps, pns, ls, ms = [], [], [], []
    for (bb, hg), q8, s8 in zip(items, q8s, s8s):
        sn = jnp.sum(q8 * k_ref[bb, hsl(hg), :], axis=-1, keepdims=True) * scale
        m = jnp.maximum(jnp.max(s8, axis=-1, keepdims=True), sn)
        p = jnp.exp(s8 - m)
        pn = jnp.exp(sn - m)
        ps.append(p)
        pns.append(pn)
        ms.append(m)
        ls.append(jnp.sum(p, axis=-1, keepdims=True) + pn)
    for (bb, hg), p, pn, l, m in zip(items, ps, pns, ls, ms):
        o8 = jnp.zeros((8, ATT_DH), F32)
        for hh in range(8):
            o8 = jnp.where(r8 == hh, _dot_nt(p, c_ref[bb, 1, 8 * hg + hh]), o8)
        o_ref[bb, hsl(hg), :] = (o8 + pn * v_ref[bb, hsl(hg), :]) * (1.0 / l)
        lse_ref[bb, hsl(hg), :] = jnp.broadcast_to(m + jnp.log(l), (8, ATT_DH))


def _attn_step(q3, k3, v3, cache, layer, gi):
    Bs = q3.shape[0]
    window, dil = DIL_GROUPS[gi]
    L = cache.shape[2]
    assert L == window and L % dil == 0 and dil & (dil - 1) == 0
    ct = jnp.transpose(cache, (0, 1, 3, 4, 5, 2))
    R = max(1, min(8, ATT_STEP_CACHE_BLOCK_BYTES // (2 * ATT_DIM * L * 4)))
    assert Bs % R == 0
    rspec = pl.BlockSpec((R, ATT_H, ATT_DH), lambda i: (i, 0, 0))
    return pl.pallas_call(
        functools.partial(_attn_step_body, rows=R, dil=dil),
        out_shape=(jax.ShapeDtypeStruct((Bs, ATT_H, ATT_DH), F32),
                   jax.ShapeDtypeStruct((Bs, ATT_H, ATT_DH), F32)),
        grid=(Bs // R,),
        in_specs=[rspec, rspec, rspec,
                  pl.BlockSpec((None, R, 2, ATT_H, ATT_DH, L), lambda i: (layer, i, 0, 0, 0, 0))],
        out_specs=(rspec, rspec),
        compiler_params=_cparams(("parallel",)),
        name="attn_step",
    )(q3, k3, v3, ct)


def _merge_step_body(o0_ref, o1_ref, o2_ref, l0_ref, l1_ref, l2_ref, out_ref):
    l0, l1, l2 = l0_ref[...], l1_ref[...], l2_ref[...]
    mx = jnp.maximum(jnp.maximum(l0, l1), l2)
    e0, e1, e2 = jnp.exp(l0 - mx), jnp.exp(l1 - mx), jnp.exp(l2 - mx)
    inv = 1.0 / (e0 + e1 + e2)
    out_ref[...] = (e0 * inv) * o0_ref[...] + (e1 * inv) * o1_ref[...] + (e2 * inv) * o2_ref[...]


def _merge_step(os, lses):
    shape = os[0].shape
    spec = pl.BlockSpec(shape, lambda i: (0, 0, 0))
    return pl.pallas_call(
        _merge_step_body,
        out_shape=jax.ShapeDtypeStruct(shape, F32),
        grid=(1,),
        in_specs=[spec] * 6,
        out_specs=spec,
        compiler_params=_cparams(("arbitrary",)),
        name="attn_merge_step",
    )(*os, *lses)


def _router_body(h_ref, w_ref, b_ref, o_ref):
    logits = _dot3(h_ref[...].astype(F32), w_ref[...]) + b_ref[...]
    lane = _iota(logits.shape, 1)
    logits = jnp.where(lane < N_EXPERTS, logits, -jnp.inf)
    m1 = jnp.max(logits, axis=-1, keepdims=True)
    i1 = jnp.min(jnp.where(logits == m1, lane, LANES), axis=-1, keepdims=True)
    rest = jnp.where(lane == i1, -jnp.inf, logits)
    m2 = jnp.max(rest, axis=-1, keepdims=True)
    i2 = jnp.min(jnp.where(rest == m2, lane, LANES), axis=-1, keepdims=True)
    e2 = jnp.exp(m2 - m1)
    g1 = 1.0 / (1.0 + e2)
    g2 = e2 * g1
    o_ref[...] = jnp.where(lane == i1, g1, 0.0) + jnp.where(lane == i2, g2, 0.0)


def _router(h, w_router, b_router):
    M, D = h.shape
    tm = min(M, 1024)
    wp = jnp.pad(w_router, ((0, 0), (0, LANES - N_EXPERTS)))
    bp = jnp.pad(b_router, (0, LANES - N_EXPERTS)).reshape(1, LANES)
    return pl.pallas_call(
        _router_body,
        out_shape=jax.ShapeDtypeStruct((M, LANES), F32),
        grid=(M // tm,),
        in_specs=[pl.BlockSpec((tm, D), lambda i: (i, 0)),
                  pl.BlockSpec((D, LANES), lambda i: (0, 0)),
                  pl.BlockSpec((1, LANES), lambda i: (0, 0))],
        out_specs=pl.BlockSpec((tm, LANES), lambda i: (i, 0)),
        compiler_params=_cparams(("parallel",)),
        name="router",
    )(h, wp, bp)


MOE_BLOCK = 512
MOE_TILE = 512
MOE_WINDOW = 128
MOE_NOT_ROUTED = -float(1 << 20)


def _router_t_body(h_ref, wt_ref, b_ref, comb_ref, rank_ref, cnt_ref):
    wh, wl = _split2(wt_ref[...])
    hb = h_ref[...]
    logits = _dot_nt(wh, hb) + _dot_nt(wl, hb) + b_ref[...]
    row = _iota(logits.shape, 0)
    m1 = jnp.max(logits, axis=0, keepdims=True)
    i1 = jnp.min(jnp.where(logits == m1, row, N_EXPERTS), axis=0, keepdims=True)
    rest = jnp.where(row == i1, -jnp.inf, logits)
    m2 = jnp.max(rest, axis=0, keepdims=True)
    i2 = jnp.min(jnp.where(rest == m2, row, N_EXPERTS), axis=0, keepdims=True)
    e2 = jnp.exp(m2 - m1)
    g1 = 1.0 / (1.0 + e2)
    g2 = e2 * g1
    routed = (row == i1) | (row == i2)
    comb_ref[...] = jnp.where(row == i1, g1, 0.0) + jnp.where(row == i2, g2, 0.0)
    onehot = jnp.where(routed, 1.0, 0.0)
    tb = onehot.shape[1]
    before = (_iota((tb, tb), 0) < _iota((tb, tb), 1)).astype(BF16)
    rank = _dot(onehot, before)
    rank_ref[...] = jnp.where(routed, rank, MOE_NOT_ROUTED)
    cnt_ref[...] = jnp.broadcast_to(jnp.sum(onehot, axis=1, keepdims=True), cnt_ref.shape)


def _router_t(h, w_router, b_router):
    M, D = h.shape
    tb = MOE_BLOCK
    nblk = M // tb
    return pl.pallas_call(
        _router_t_body,
        out_shape=(jax.ShapeDtypeStruct((N_EXPERTS, M), F32), jax.ShapeDtypeStruct((N_EXPERTS, M), F32),
                   jax.ShapeDtypeStruct((nblk, N_EXPERTS, LANES), F32)),
        grid=(nblk,),
        in_specs=[pl.BlockSpec((tb, D), lambda i: (i, 0)),
                  pl.BlockSpec((N_EXPERTS, D), lambda i: (0, 0)),
                  pl.BlockSpec((N_EXPERTS, 1), lambda i: (0, 0))],
        out_specs=(pl.BlockSpec((N_EXPERTS, tb), lambda i: (0, i)),
                   pl.BlockSpec((N_EXPERTS, tb), lambda i: (0, i)),
                   pl.BlockSpec((None, N_EXPERTS, LANES), lambda i: (i, 0, 0))),
        compiler_params=_cparams(("parallel",)),
        name="router_t",
    )(h, jnp.transpose(w_router), b_router.reshape(N_EXPERTS, 1))


def _moe_work_lists(cnt, n_tiles):
    ts = MOE_TILE
    n_e, nblk = cnt.shape
    base = jnp.cumsum(cnt, axis=1) - cnt
    ntile = (jnp.sum(cnt, axis=1) + ts - 1) // ts
    tend = jnp.cumsum(ntile)
    tstart = tend - ntile
    q_lo = base // ts
    q_hi = (base + jnp.maximum(cnt, 1) - 1) // ts
    q = jnp.stack([q_lo, q_hi], axis=-1)
    valid = jnp.stack([cnt > 0, (cnt > 0) & (q_hi > q_lo)], axis=-1)
    tile = tstart[:, None, None] + q
    delta = q * ts - base[:, :, None]
    e_idx = jnp.broadcast_to(jnp.arange(n_e, dtype=jnp.int32)[:, None, None], q.shape)
    j_idx = jnp.broadcast_to(jnp.arange(nblk, dtype=jnp.int32)[None, :, None], q.shape)

    row0 = jnp.maximum(-delta, 0)
    row1 = jnp.minimum(cnt[:, :, None] - delta, ts)
    wbits = jnp.zeros_like(q)
    for w in range(ts // MOE_WINDOW):
        touched = (row0 < (w + 1) * MOE_WINDOW) & (row1 > w * MOE_WINDOW)
        wbits = wbits + touched.astype(q.dtype) * (4 << w)

    def ordered(perm, key):
        items = jnp.stack([jnp.transpose(a, perm).reshape(-1).astype(jnp.int32)
                           for a in (tile, j_idx, e_idx, delta, wbits, valid)])
        order = jnp.argsort(jnp.logical_not(items[5] > 0), stable=True)
        n_valid = jnp.sum(items[5])
        n_items = order.shape[0]
        pos = jnp.minimum(jnp.arange(n_items), jnp.maximum(n_valid - 1, 0))
        tile_s, j_s, e_s, d_s, w_s, _ = items[:, order[pos]]
        live = (jnp.arange(n_items) < n_valid).astype(jnp.int32)
        k = tile_s if key == "tile" else j_s
        first = jnp.concatenate([jnp.ones((1,), jnp.int32), (k[1:] != k[:-1]).astype(jnp.int32)])
        return tile_s, j_s, e_s, d_s, first + live * (2 + w_s)

    gather_list = ordered((0, 1, 2), "tile")
    combine_list = ordered((1, 0, 2), "blk")
    tile_expert = jnp.minimum(jnp.sum(jnp.arange(n_tiles)[:, None] >= tend[None, :], axis=1),
                              n_e - 1).astype(jnp.int32)
    return gather_list, combine_list, tile_expert


def _slot_selector(rank_ref, e, delta, w):
    tgt = rank_ref[pl.ds(e, 1), :] - (delta + w * MOE_WINDOW).astype(F32)
    r = _iota((MOE_WINDOW, tgt.shape[1]), 0).astype(F32)
    return jnp.where(r == tgt, 1.0, 0.0).astype(BF16)


def _moe_gather_body(tile_ref, blk_ref, e_ref, dlt_ref, flg_ref, h_ref, rank_ref, zero_ref, o_ref, acc_scr):
    s = pl.program_id(0)
    flags = flg_ref[s]

    @pl.when((flags & 1) == 1)
    def _():
        acc_scr[...] = jnp.zeros_like(acc_scr)

    for w in range(MOE_TILE // MOE_WINDOW):
        @pl.when((flags & (4 << w)) != 0)
        def _():
            sel = _slot_selector(rank_ref, e_ref[s], dlt_ref[s], w)
            rows = slice(w * MOE_WINDOW, (w + 1) * MOE_WINDOW)
            acc_scr[rows, :] += jnp.dot(sel, h_ref[...], preferred_element_type=F32)

    o_ref[...] = acc_scr[...].astype(o_ref.dtype)


def _moe_gather(h, rank_t, work, n_tiles):
    M, D = h.shape
    tb, ts = MOE_BLOCK, MOE_TILE
    n_items = work[0].shape[0]
    zeros = jnp.zeros((n_tiles * ts, D), h.dtype)
    return pl.pallas_call(
        _moe_gather_body,
        out_shape=jax.ShapeDtypeStruct((n_tiles * ts, D), h.dtype),
        grid_spec=pltpu.PrefetchScalarGridSpec(
            num_scalar_prefetch=5, grid=(n_items,),
            in_specs=[pl.BlockSpec((tb, D), lambda s, t, b, e, d, f: (b[s], 0)),
                      pl.BlockSpec((N_EXPERTS, tb), lambda s, t, b, e, d, f: (0, b[s])),
                      pl.BlockSpec(memory_space=pl.ANY)],
            out_specs=pl.BlockSpec((ts, D), lambda s, t, b, e, d, f: (t[s], 0)),
            scratch_shapes=[pltpu.VMEM((ts, D), F32)]),
        input_output_aliases={7: 0},
        compiler_params=_cparams(("arbitrary",)),
        name="moe_gather",
    )(*work, h, rank_t, zeros)


def _moe_combine_body(tile_ref, blk_ref, e_ref, dlt_ref, flg_ref, ys_ref, rank_ref, comb_ref, x_ref, gate_ref,
                      o_ref, acc_scr):
    s = pl.program_id(0)
    flags = flg_ref[s]

    @pl.when((flags & 1) == 1)
    def _():
        acc_scr[...] = jnp.zeros_like(acc_scr)

    for w in range(MOE_TILE // MOE_WINDOW):
        @pl.when((flags & (4 << w)) != 0)
        def _():
            e = e_ref[s]
            sel = _slot_selector(rank_ref, e, dlt_ref[s], w)
            c3 = jnp.concatenate(_split3(comb_ref[pl.ds(e, 1), :]) + (jnp.zeros((5, sel.shape[1]), BF16),),
                                 axis=0)
            g = lax.dot_general(sel, c3, (((1,), (1,)), ((), ())), preferred_element_type=F32)
            gate_col = g[:, 0:1] + g[:, 1:2] + g[:, 2:3]
            ysg = ys_ref[w * MOE_WINDOW:(w + 1) * MOE_WINDOW, :] * gate_col
            acc_scr[...] += _dot_tn(sel, ysg)

    o_ref[...] = x_ref[...] + gate_ref[...] * acc_scr[...]


def _moe_combine(ys, rank_t, comb_t, x, gate, work, rows_per_group):
    M, D = x.shape
    tb, ts = MOE_BLOCK, MOE_TILE
    n_items = work[0].shape[0]
    assert rows_per_group % tb == 0
    espec = pl.BlockSpec((N_EXPERTS, tb), lambda s, t, b, e, d, f: (0, b[s]))
    xspec = pl.BlockSpec((tb, D), lambda s, t, b, e, d, f: (b[s], 0))
    return pl.pallas_call(
        _moe_combine_body,
        out_shape=jax.ShapeDtypeStruct((M, D), F32),
        grid_spec=pltpu.PrefetchScalarGridSpec(
            num_scalar_prefetch=5, grid=(n_items,),
            in_specs=[pl.BlockSpec((ts, D), lambda s, t, b, e, d, f: (t[s], 0)),
                      espec, espec, xspec,
                      pl.BlockSpec((None, 1, D), lambda s, t, b, e, d, f: ((b[s] * tb) // rows_per_group, 0, 0))],
            out_specs=xspec,
            scratch_shapes=[pltpu.VMEM((tb, D), F32)]),
        compiler_params=_cparams(("arbitrary",)),
        name="moe_combine",
    )(*work, ys, rank_t, comb_t, x, gate.reshape(gate.shape[0], 1, D))


def _moe_top2(h, x, gate, w_router, b_router, w_gate, w_up, w_down, layer, rpg):
    M, D = x.shape
    n_tiles = 2 * M // MOE_TILE + N_EXPERTS
    comb_t, rank_t, cnt = _router_t(h, w_router, b_router)
    gather_list, combine_list, tile_expert = _moe_work_lists(
        jnp.transpose(cnt[:, :, 0]).astype(jnp.int32), n_tiles)
    xs = _moe_gather(h, rank_t, gather_list, n_tiles)
    act = _mm_grouped(xs, w_gate, layer, tile_expert, MOE_TILE, out_dtype=BF16, w2=w_up, name="moe_up")
    ys = _mm_grouped(act, w_down, layer, tile_expert, MOE_TILE, out_dtype=F32, name="moe_down")
    return _moe_combine(ys, rank_t, comb_t, x, gate, combine_list, rpg)


def _moe(h, x, gate, comb, w_gate, w_up, w_down, layer, rpg):
    d_ff = w_gate.shape[-1]
    D = x.shape[1]
    acc = None
    for e in range(N_EXPERTS):
        act = _mm(h, w_gate, (layer, e), d_ff, out_dtype=BF16, w2=w_up, name="moe_up")
        last = e == N_EXPERTS - 1
        acc = _mm(act, w_down, (layer, e), D, out_dtype=F32, rowscale=comb, rowscale_col=e, res=acc,
                  gate=gate if last else None, res2=x if last else None, rows_per_group=rpg,
                  name="moe_down")
    return acc


def _trunk(x, mod, rpg, Bn, T, is_prompt, state_gdn, state_conv_t, caches, invf, p):
    D = x.shape[1]
    d_ff = p["w_gate_d"].shape[-1]
    new_s, new_conv = [], []
    new_kv = [[] for _ in range(N_GROUPS)]
    for i in range(DEPTH):
        j = i // 2
        sh_m, sc_m, gt_m, sh_f, sc_f, gt_f = [mod[i][:, n * D:(n + 1) * D] for n in range(6)]
        h = _norm_mod(x, p["norm_mix"][i], sh_m, sc_m, rpg, BF16 if is_prompt else F32)
        if i % 2 == 0:
            proj = _mm(h, p["w_in_a"], (j,), GDN_QKV_DIM + GDN_VAL_DIM, out_dtype=F32, name="gdn_in")
            ab = _mm_small(h, p["w_in_a"][j][:, GDN_QKV_DIM + GDN_VAL_DIM:], "gdn_in_ab")
            if is_prompt:
                prep = _gdn_prep(proj, ab, p["conv_w_a"], p["a_log"], p["dt_bias"], j, Bn, T)
                og, S = _gdn_scan(prep, proj, p["norm_o_a"], j, Bn, T)
                tail = proj.reshape(Bn, T, -1)[:, T - (CONV_W - 1):, :GDN_QKV_DIM]
                new_conv.append(tail)
            else:
                og, cb_t, S = _gdn_step(proj, ab, state_conv_t, p["conv_w_a"], p["a_log"], p["dt_bias"],
                                        p["norm_o_a"], state_gdn, j)
                new_conv.append(jnp.transpose(cb_t, (1, 0, 2)))
            new_s.append(S)
            x = _mm(og, p["w_out_a"], (j,), D, out_dtype=F32, gate=gt_m, res2=x, rows_per_group=rpg,
                    name="gdn_out")
        else:
            proj = _mm(h, p["w_in_b"], (j,), 3 * N_GROUPS * ATT_DIM, out_dtype=F32, name="att_in")
            if is_prompt:
                qkvs = _rope_split(proj, invf, Bn, T)
                os, lses = [], []
                for gi, (window, dil) in enumerate(DIL_GROUPS):
                    o_g, lse_g = _attn_prompt(qkvs[gi], gi, Bn, T)
                    os.append(o_g)
                    lses.append(lse_g)
                    wn = min(window, T)
                    k_tail = _rope(proj, invf, Bn, T, T - wn, wn, 3 * gi + 1, 1, F32)
                    v_tail = proj.reshape(Bn, T, -1)[:, T - wn:, (3 * gi + 2) * ATT_DIM:(3 * gi + 3) * ATT_DIM]
                    new_kv[gi].append(jnp.stack(
                        [k_tail.reshape(Bn, wn, ATT_H, ATT_DH), v_tail.reshape(Bn, wn, ATT_H, ATT_DH)], axis=2))
                om = _merge_groups(os, lses, Bn, T, BF16)
            else:
                Bs = x.shape[0]
                pr = _rope(proj, invf, 1, Bs, 0, Bs, 0, 3 * N_GROUPS, F32, pos0=PAST_LEN, pos_step=0)
                os, lses = [], []
                for gi in range(N_GROUPS):
                    q3, k3, v3 = [pr[:, (3 * gi + n) * ATT_DIM:(3 * gi + n + 1) * ATT_DIM]
                                  .reshape(Bs, ATT_H, ATT_DH) for n in range(3)]
                    o_g, lse_g = _attn_step(q3, k3, v3, caches[gi], j, gi)
                    os.append(o_g)
                    lses.append(lse_g)
                    new_kv[gi].append(jnp.stack([k3, v3], axis=1).reshape(Bs, 1, 2, ATT_H, ATT_DH))
                om = _merge_step(os, lses).reshape(Bs, ATT_DIM)
            x = _mm(om, p["w_out_b"], (j,), D, out_dtype=F32, gate=gt_m, res2=x, rows_per_group=rpg,
                    name="att_out")
        h = _norm_mod(x, p["norm_ffn"][i], sh_f, sc_f, rpg, BF16 if is_prompt else F32)
        if i % 2 == 0:
            act = _mm(h, p["w_gate_d"], (j,), d_ff, out_dtype=BF16, w2=p["w_up_d"], name="ffn_up")
            x = _mm(act, p["w_down_d"], (j,), D, out_dtype=F32, gate=gt_f, res2=x, rows_per_group=rpg,
                    name="ffn_down")
        elif is_prompt:
            x = _moe_top2(h, x, gt_f, p["w_router"][j], p["b_router"][j],
                          p["w_gate_e"], p["w_up_e"], p["w_down_e"], j, rpg)
        else:
            comb = _router(h, p["w_router"][j], p["b_router"][j])
            x = _moe(h, x, gt_f, comb, p["w_gate_e"], p["w_up_e"], p["w_down_e"], j, rpg)
    y = _norm_mod(x, p["norm_final"], None, None, rpg, F32)
    return (y, jnp.stack(new_s), jnp.stack(new_conv),
            jnp.stack(new_kv[0]), jnp.stack(new_kv[1]), jnp.stack(new_kv[2]))


def kernel(x_prompt, x_sample, state_gdn, state_conv, cache_kv_w128, cache_kv_w512, cache_kv_w2048,
           c_prompt, c_sample, w_ada, b_ada, norm_mix, norm_ffn, norm_final,
           w_in_a, conv_w_a, a_log, dt_bias, norm_o_a, w_out_a, w_in_b, w_out_b,
           w_gate_d, w_up_d, w_down_d, w_router, b_router, w_gate_e, w_up_e, w_down_e):
    Bp, T, D = x_prompt.shape
    Bs, Ts, _ = x_sample.shape
    assert Ts == 1 and cache_kv_w128.shape[2] == DIL_GROUPS[0][0]
    p = dict(norm_mix=norm_mix, norm_ffn=norm_ffn, norm_final=norm_final, w_in_a=w_in_a, conv_w_a=conv_w_a,
             a_log=a_log, dt_bias=dt_bias, norm_o_a=norm_o_a, w_out_a=w_out_a, w_in_b=w_in_b, w_out_b=w_out_b,
             w_gate_d=w_gate_d, w_up_d=w_up_d, w_down_d=w_down_d, w_router=w_router, b_router=b_router,
             w_gate_e=w_gate_e, w_up_e=w_up_e, w_down_e=w_down_e)

    n_c = Bp + Bs
    pad = (-n_c) % 8
    c_all = jnp.concatenate([c_prompt, c_sample, jnp.zeros((pad, D), F32)], axis=0)
    mods = [_mm(c_all, w_ada, (l,), 6 * D, out_dtype=F32, pre_silu=True, res=b_ada[l].reshape(1, 6 * D),
                rows_per_group=n_c + pad, name="adaln") for l in range(DEPTH)]
    mod_p = jnp.stack([m[:Bp] for m in mods])
    mod_s = jnp.stack([m[Bp:n_c] for m in mods])

    half = ATT_DH // 2
    inv_freq = ROPE_THETA ** (-jnp.arange(half, dtype=F32) / half)
    invf = jnp.tile(inv_freq, LANES // half).reshape(1, LANES)

    y_p, s_p, cv_p, kv0_p, kv1_p, kv2_p = _trunk(
        x_prompt.reshape(Bp * T, D), mod_p, T, Bp, T, True, None, None, None, invf, p)
    y_s, s_s, cv_s, kv0_s, kv1_s, kv2_s = _trunk(
        x_sample.reshape(Bs, D), mod_s, 1, Bs, 1, False, state_gdn,
        jnp.transpose(state_conv, (0, 2, 1, 3)), (cache_kv_w128, cache_kv_w512, cache_kv_w2048), invf, p)
    return (y_p.reshape(Bp, T, D), y_s.reshape(Bs, Ts, D), s_p, cv_p, kv0_p, kv1_p, kv2_p,
            s_s, cv_s, kv0_s, kv1_s, kv2_s)
```

```python
import functools
import math

import jax
import jax.numpy as jnp
from jax import lax
from jax.experimental import pallas as pl
from jax.experimental.pallas import tpu as pltpu

F32 = jnp.float32
BF16 = jnp.bfloat16

DEPTH = 4
GDN_HK = 8
GDN_HV = 16
GDN_DK = 128
GDN_DV = 128
GDN_KEY_DIM = GDN_HK * GDN_DK
GDN_VAL_DIM = GDN_HV * GDN_DV
GDN_QKV_DIM = 2 * GDN_KEY_DIM + GDN_VAL_DIM
CONV_W = 4
GDN_CHUNK = 64
ATT_H = 16
ATT_DH = 64
ATT_DIM = ATT_H * ATT_DH
DIL_GROUPS = ((128, 1), (512, 4), (2048, 16))
N_GROUPS = len(DIL_GROUPS)
ATT_BLOCK = 128
ROPE_THETA = 10000.0
N_EXPERTS = 8
EPS = 1e-6
NEG_INF = -1e30
PAST_LEN = 2048

V7X_VMEM_BYTES = 64 * 1024 * 1024
VMEM_LIMIT_BYTES = 52 * 1024 * 1024
VMEM_BLOCK_BUDGET = 40 * 1024 * 1024
LANES = 128


def _cparams(sem):
    return pltpu.CompilerParams(dimension_semantics=sem, vmem_limit_bytes=VMEM_LIMIT_BYTES)


def _dot(a, b):
    return jnp.dot(a.astype(BF16), b.astype(BF16), preferred_element_type=F32)


def _dot_nt(a, b):
    return lax.dot_general(a.astype(BF16), b.astype(BF16), (((1,), (1,)), ((), ())),
                           preferred_element_type=F32)


def _dot_tn(a, b):
    return lax.dot_general(a.astype(BF16), b.astype(BF16), (((0,), (0,)), ((), ())),
                           preferred_element_type=F32)


def _split2(a):
    hi = a.astype(BF16)
    lo = (a - hi.astype(F32)).astype(BF16)
    return hi, lo


def _split3(a):
    hi = a.astype(BF16)
    r = a - hi.astype(F32)
    mid = r.astype(BF16)
    lo = (r - mid.astype(F32)).astype(BF16)
    return hi, mid, lo


def _dot3(a, b):
    ah, al = _split2(a)
    bh, bl = _split2(b)
    return _dot(ah, bh) + (_dot(ah, bl) + _dot(al, bh))


def _dot_exact_lhs(c, b, passes=3):
    parts = _split3(b) if passes == 3 else _split2(b)
    acc = _dot(c, parts[0])
    for p in parts[1:]:
        acc = acc + _dot(c, p)
    return acc


def _dot_exact_rhs(a, c, passes=3):
    parts = _split3(a) if passes == 3 else _split2(a)
    acc = _dot(parts[0], c)
    for p in parts[1:]:
        acc = acc + _dot(p, c)
    return acc


def _silu(x):
    return x * jax.nn.sigmoid(x)


def _iota(shape, dim):
    return lax.broadcasted_iota(jnp.int32, shape, dim)


def _norm_mod_body(x_ref, nw_ref, *rest, modulate):
    if modulate:
        sh_ref, sc_ref, o_ref = rest
    else:
        (o_ref,) = rest
    x = x_ref[...].astype(F32)
    y = x * lax.rsqrt(jnp.mean(x * x, axis=-1, keepdims=True) + EPS) * nw_ref[...]
    if modulate:
        y = y * (1.0 + sc_ref[...]) + sh_ref[...]
    o_ref[...] = y.astype(o_ref.dtype)


def _norm_mod(x, nw, shift, scale, rows_per_group, out_dtype):
    M, D = x.shape
    tm = min(M, 1024)
    assert M % tm == 0
    modulate = shift is not None
    in_specs = [pl.BlockSpec((tm, D), lambda i: (i, 0)), pl.BlockSpec((1, D), lambda i: (0, 0))]
    args = [x, nw.reshape(1, D)]
    if modulate:
        if rows_per_group == 1:
            spec = pl.BlockSpec((tm, D), lambda i: (i, 0))
            args += [shift, scale]
        else:
            assert rows_per_group % tm == 0
            spec = pl.BlockSpec((None, 1, D), lambda i: ((i * tm) // rows_per_group, 0, 0))
            args += [shift.reshape(-1, 1, D), scale.reshape(-1, 1, D)]
        in_specs += [spec, spec]
    return pl.pallas_call(
        functools.partial(_norm_mod_body, modulate=modulate),
        out_shape=jax.ShapeDtypeStruct((M, D), out_dtype),
        grid=(M // tm,),
        in_specs=in_specs,
        out_specs=pl.BlockSpec((tm, D), lambda i: (i, 0)),
        compiler_params=_cparams(("parallel",)),
        name="norm_mod",
    )(*args)


def _mm_body(*refs, pre_silu, swiglu, has_rowscale, rowscale_col, has_res, has_gate, has_res2):
    it = iter(refs)
    x_ref = next(it)
    w_ref = next(it)
    w2_ref = next(it) if swiglu else None
    rs_ref = next(it) if has_rowscale else None
    res_ref = next(it) if has_res else None
    gate_ref = next(it) if has_gate else None
    res2_ref = next(it) if has_res2 else None
    o_ref = next(it)
    x = x_ref[...]
    if pre_silu:
        x = _silu(x.astype(F32))
    xb = x.astype(BF16)
    acc = jnp.dot(xb, w_ref[...].astype(BF16), preferred_element_type=F32)
    if swiglu:
        acc = _silu(acc) * jnp.dot(xb, w2_ref[...].astype(BF16), preferred_element_type=F32)
    if has_rowscale:
        acc = acc * rs_ref[...][:, rowscale_col:rowscale_col + 1]
    if has_res:
        acc = acc + res_ref[...]
    if has_gate:
        acc = acc * gate_ref[...]
    if has_res2:
        acc = acc + res2_ref[...]
    o_ref[...] = acc.astype(o_ref.dtype)


def _pick_tiles(M, K, N, x_bytes, n_w, out_bytes, n_res):
    def est(tm_, tn_):
        blocks = tm_ * K * x_bytes + n_w * K * tn_ * 4 + tm_ * tn_ * (out_bytes + 4 * n_res)
        temps = n_w * K * tn_ * 2 + tm_ * K * 2 + (1 + n_w) * tm_ * tn_ * 4
        return 2 * blocks + temps

    tms = [t for t in (1024, 512, 256, 128) if t <= M and M % t == 0] or [M]
    tns = [t for t in (1024, 896, 768, 512, 384, 256, 128) if N % t == 0] or [N]
    wide = [(a, b) for a in tms for b in tns if a >= 512 and b >= 512]
    rest = [(a, b) for a in tms for b in tns if (a, b) not in wide]
    for tm, tn in wide + rest:
        if est(tm, tn) <= VMEM_BLOCK_BUDGET:
            return tm, tn
    return tms[-1], tns[-1]


def _mm(x, w, lead, n_cols, col0=0, *, out_dtype, w2=None, pre_silu=False, rowscale=None,
        rowscale_col=0, res=None, gate=None, res2=None, rows_per_group=1, name="mm"):
    M, K = x.shape
    swiglu = w2 is not None
    n_res = int(res is not None) + int(res2 is not None) + int(gate is not None)
    tm, tn = _pick_tiles(M, K, n_cols, x.dtype.itemsize, 2 if swiglu else 1,
                         jnp.dtype(out_dtype).itemsize, n_res)
    assert col0 % tn == 0 and n_cols % tn == 0
    nj, ni = n_cols // tn, M // tm
    jb0 = col0 // tn
    nlead = len(lead)
    wblock = (None,) * nlead + (K, tn)

    def wmap(j, i):
        return tuple(lead) + (0, jb0 + j)

    in_specs = [pl.BlockSpec((tm, K), lambda j, i: (i, 0)), pl.BlockSpec(wblock, wmap)]
    args = [x, w]
    if swiglu:
        in_specs.append(pl.BlockSpec(wblock, wmap))
        args.append(w2)
    if rowscale is not None:
        in_specs.append(pl.BlockSpec((tm, rowscale.shape[1]), lambda j, i: (i, 0)))
        args.append(rowscale)

    def tile_spec(a):
        if a.shape[0] == M:
            return pl.BlockSpec((tm, tn), lambda j, i: (i, j)), a
        assert rows_per_group % tm == 0 and a.shape[0] * rows_per_group == M
        return (pl.BlockSpec((None, 1, tn), lambda j, i: ((i * tm) // rows_per_group, 0, j)),
                a.reshape(a.shape[0], 1, a.shape[1]))

    for a in (res, gate, res2):
        if a is not None:
            s, a2 = tile_spec(a)
            in_specs.append(s)
            args.append(a2)
    body = functools.partial(
        _mm_body, pre_silu=pre_silu, swiglu=swiglu, has_rowscale=rowscale is not None,
        rowscale_col=rowscale_col, has_res=res is not None, has_gate=gate is not None,
        has_res2=res2 is not None)
    return pl.pallas_call(
        body,
        out_shape=jax.ShapeDtypeStruct((M, n_cols), out_dtype),
        grid=(nj, ni),
        in_specs=in_specs,
        out_specs=pl.BlockSpec((tm, tn), lambda j, i: (i, j)),
        compiler_params=_cparams(("parallel", "parallel")),
        name=name,
    )(*args)


def _mm_grouped(x, w, layer, tile_expert, tm, *, out_dtype, w2=None, name="mm_grouped"):
    M, K = x.shape
    N = w.shape[-1]
    swiglu = w2 is not None
    tn = next(t for t in (896, 512, 256, 128) if N % t == 0)
    nj, ni = N // tn, M // tm
    wspec = pl.BlockSpec((None, None, K, tn), lambda j, i, te: (layer, te[i], 0, j))
    in_specs = [pl.BlockSpec((tm, K), lambda j, i, te: (i, 0)), wspec]
    args = [x, w]
    if swiglu:
        in_specs.append(wspec)
        args.append(w2)
    inner = functools.partial(_mm_body, pre_silu=False, swiglu=swiglu, has_rowscale=False, rowscale_col=0,
                              has_res=False, has_gate=False, has_res2=False)

    def body(te_ref, *refs):
        inner(*refs)

    return pl.pallas_call(
        body,
        out_shape=jax.ShapeDtypeStruct((M, N), out_dtype),
        grid_spec=pltpu.PrefetchScalarGridSpec(
            num_scalar_prefetch=1, grid=(nj, ni), in_specs=in_specs,
            out_specs=pl.BlockSpec((tm, tn), lambda j, i, te: (i, j))),
        compiler_params=_cparams(("parallel", "parallel")),
        name=name,
    )(tile_expert, *args)


def _mm_small(x, w_small, name):
    M, K = x.shape
    n = w_small.shape[1]
    tm = min(M, 1024)

    def body(x_ref, w_ref, o_ref):
        o_ref[...] = _dot(x_ref[...], w_ref[...])

    return pl.pallas_call(
        body,
        out_shape=jax.ShapeDtypeStruct((M, n), F32),
        grid=(M // tm,),
        in_specs=[pl.BlockSpec((tm, K), lambda i: (i, 0)), pl.BlockSpec((K, n), lambda i: (0, 0))],
        out_specs=pl.BlockSpec((tm, n), lambda i: (i, 0)),
        compiler_params=_cparams(("parallel",)),
        name=name,
    )(x, w_small)


GDN_PREP_ROWS = 512
GDN_PREP_CHUNKS = GDN_PREP_ROWS // GDN_CHUNK
PAIR = 2 * GDN_CHUNK


def _gdn_gates(ab, alog, dtb):
    a = ab[:, :GDN_HV]
    b = ab[:, GDN_HV:]
    xg = a + dtb
    softplus = jnp.maximum(xg, 0.0) + jnp.log1p(jnp.exp(-jnp.abs(xg)))
    return -jnp.exp(alog) * softplus, jax.nn.sigmoid(b)


def _gdn_prep_body(q_ref, qp_ref, k_ref, kp_ref, v_ref, vp_ref, cwq_ref, cwk_ref, cwv_ref,
                   ab_ref, alog_ref, dtb_ref,
                   u_ref, w_ref, qd_ref, kd_ref, qk_ref, gl_ref):
    i = pl.program_id(1)
    hk = pl.program_id(2)
    rows = GDN_PREP_ROWS
    first = i == 0

    def conv(x_ref, xp_ref, cw_ref):
        cur = x_ref[...]
        prev = jnp.where(first, 0.0, xp_ref[...])
        ext = jnp.concatenate([prev, cur], axis=0)
        cw = cw_ref[...]
        y = cur * cw[CONV_W - 1:CONV_W, :]
        for s in range(1, CONV_W):
            y = y + ext[8 - s:8 - s + rows, :] * cw[CONV_W - 1 - s:CONV_W - s, :]
        return _silu(y)

    qc = conv(q_ref, qp_ref, cwq_ref)
    kc = conv(k_ref, kp_ref, cwk_ref)
    vc = conv(v_ref, vp_ref, cwv_ref)
    qn = qc * lax.rsqrt(jnp.sum(qc * qc, axis=-1, keepdims=True) + EPS) * (GDN_DK ** -0.5)
    kn = kc * lax.rsqrt(jnp.sum(kc * kc, axis=-1, keepdims=True) + EPS)

    g16, be16 = _gdn_gates(ab_ref[...], alog_ref[...], dtb_ref[...])
    r16 = _iota((GDN_HV, LANES), 0)

    def head_bcast(x16, hv):
        return _dot_exact_rhs(x16, (r16 == hv).astype(BF16))

    g_b = [head_bcast(g16, 2 * hk + s) for s in (0, 1)]
    be_b = [head_bcast(be16, 2 * hk + s) for s in (0, 1)]

    ri = _iota((PAIR, PAIR), 0)
    ci = _iota((PAIR, PAIR), 1)
    same = (ri >> 6) == (ci >> 6)
    causal = same & (ri >= ci)
    strict = same & (ri > ci)
    blk16 = (ri >> 4) == (ci >> 4)
    tri = causal.astype(BF16)
    eye = (ri == ci).astype(F32)
    r8 = _iota((8, LANES), 0)

    chunks = range(GDN_PREP_CHUNKS)
    sls = [slice(c * GDN_CHUNK, (c + 1) * GDN_CHUNK) for c in chunks]
    k2 = [jnp.concatenate([kn[sl], kn[sl]], axis=0) for sl in sls]
    q2 = [jnp.concatenate([qn[sl], qn[sl]], axis=0) for sl in sls]
    g2 = [jnp.concatenate([g_b[0][sl], g_b[1][sl]], axis=0) for sl in sls]
    b2 = [jnp.concatenate([be_b[0][sl], be_b[1][sl]], axis=0) for sl in sls]

    res = [_dot_exact_lhs(tri, jnp.concatenate([jnp.where(strict, g, 0.0), g], axis=1), passes=2) for g in g2]
    kk = [_dot_nt(k, k) for k in k2]
    qkr = [_dot_nt(q, k) for q, k in zip(q2, k2)]
    gcum = [r[:, PAIR:] for r in res]
    decay = [jnp.where(causal, jnp.exp(jnp.where(causal, r[:, :PAIR], 0.0)), 0.0) for r in res]
    eg = [jnp.exp(g) for g in gcum]
    lmat = [jnp.where(strict, b * k * d, 0.0) for b, k, d in zip(b2, kk, decay)]

    ld = [jnp.where(blk16, l, 0.0) for l in lmat]
    nn = [jnp.where(blk16, 0.0, l) for l in lmat]
    x = [eye - l for l in ld]
    p = [_dot(l, l) for l in ld]
    for _ in range(2):
        x = [xi + _dot(xi, pi) for xi, pi in zip(x, p)]
        p = [_dot(pi, pi) for pi in p]
    x = [xi + _dot(xi, pi) for xi, pi in zip(x, p)]
    v2 = [jnp.concatenate([vc[sl, :GDN_DV], vc[sl, GDN_DV:]], axis=0) for sl in sls]
    rhs = [jnp.concatenate([b * v, b * e * k], axis=1) for b, v, e, k in zip(b2, v2, eg, k2)]
    mm = [_dot(xi, n) for xi, n in zip(x, nn)]
    y = [_dot(xi, r) for xi, r in zip(x, rhs)]
    m2 = [_dot(m, m) for m in mm]
    y = [yi + _dot(m, yi) for yi, m in zip(y, m2)]
    y = [yi - _dot(m, yi) for yi, m in zip(y, mm)]

    for c in chunks:
        gl0 = gcum[c][GDN_CHUNK - 1:GDN_CHUNK, :]
        gl1 = gcum[c][PAIR - 1:PAIR, :]
        glast = jnp.where(ri < GDN_CHUNK, gl0, gl1)
        u_ref[c] = y[c][:, :GDN_DV]
        w_ref[c] = y[c][:, GDN_DV:].astype(BF16)
        qd_ref[c] = (q2[c] * eg[c]).astype(BF16)
        kd_ref[c] = (k2[c] * jnp.exp(glast - gcum[c])).astype(BF16)
        qk_ref[c] = (qkr[c] * decay[c]).astype(BF16)
        gl_ref[c] = jnp.where(r8 == 0, jnp.exp(gl0), jnp.where(r8 == 1, jnp.exp(gl1), 0.0))


def _gdn_prep(proj, ab, conv_w, a_log, dt_bias, layer, Bn, T):
    rows = GDN_PREP_ROWS
    nblk = T // rows
    nch = T // GDN_CHUNK
    cpb = GDN_PREP_CHUNKS

    def cur(width, col):
        return pl.BlockSpec((rows, width), lambda b, i, h: (b * nblk + i, col(h)))

    def prev(width, col):
        return pl.BlockSpec((8, width),
                            lambda b, i, h: (jnp.maximum((b * T + i * rows) // 8 - 1, 0), col(h)))

    def cw(width, col):
        return pl.BlockSpec((None, CONV_W, width), lambda b, i, h: (layer, 0, col(h)))

    qcol = lambda h: h
    kcol = lambda h: GDN_HK + h
    vcol = lambda h: GDN_HK + h
    in_specs = [cur(128, qcol), prev(128, qcol), cur(128, kcol), prev(128, kcol),
                cur(256, vcol), prev(256, vcol),
                cw(128, qcol), cw(128, kcol), cw(256, vcol),
                pl.BlockSpec((rows, 2 * GDN_HV), lambda b, i, h: (b * nblk + i, 0)),
                pl.BlockSpec((None, 1, GDN_HV), lambda b, i, h: (layer, 0, 0)),
                pl.BlockSpec((None, 1, GDN_HV), lambda b, i, h: (layer, 0, 0))]
    big = lambda dt: jax.ShapeDtypeStruct((Bn, GDN_HK, nch, PAIR, LANES), dt)
    ospec = pl.BlockSpec((None, None, cpb, PAIR, LANES), lambda b, i, h: (b, h, i, 0, 0))
    out_shape = (big(F32), big(BF16), big(BF16), big(BF16), big(BF16),
                 jax.ShapeDtypeStruct((Bn, GDN_HK, nch, 8, LANES), F32))
    out_specs = (ospec, ospec, ospec, ospec, ospec,
                 pl.BlockSpec((None, None, cpb, 8, LANES), lambda b, i, h: (b, h, i, 0, 0)))
    return pl.pallas_call(
        _gdn_prep_body,
        out_shape=out_shape,
        grid=(Bn, nblk, GDN_HK),
        in_specs=in_specs,
        out_specs=out_specs,
        compiler_params=_cparams(("parallel", "parallel", "parallel")),
        name="gdn_prep",
    )(proj, proj, proj, proj, proj, proj, conv_w, conv_w, conv_w, ab,
      a_log.reshape(-1, 1, GDN_HV), dt_bias.reshape(-1, 1, GDN_HV))


GDN_SCAN_ROWS = 256
GDN_SCAN_CHUNKS = GDN_SCAN_ROWS // GDN_CHUNK


def _gated_norm_store(o_scr, z_ref, no_ref, og_ref):
    for h in range(GDN_HV):
        sl = slice(h * GDN_DV, (h + 1) * GDN_DV)
        oh = o_scr[:, sl]
        zh = z_ref[:, sl]
        on = oh * lax.rsqrt(jnp.mean(oh * oh, axis=-1, keepdims=True) + EPS) * no_ref[...]
        og_ref[:, sl] = (on * _silu(zh)).astype(og_ref.dtype)


def _gdn_scan_body(u_ref, w_ref, qd_ref, kd_ref, qk_ref, gl_ref, z_ref, no_ref,
                   og_ref, s_ref, o_scr):
    @pl.when(pl.program_id(1) == 0)
    def _():
        s_ref[...] = jnp.zeros_like(s_ref)

    C = GDN_CHUNK
    heads = range(GDN_HV)
    for c in range(GDN_SCAN_CHUNKS):
        half = lambda ref, hv: ref[hv // 2, c][(hv % 2) * C:(hv % 2 + 1) * C]
        s = [s_ref[hv] for hv in heads]
        ws = [_dot(half(w_ref, hv), s[hv]) for hv in heads]
        qs = [_dot(half(qd_ref, hv), s[hv]) for hv in heads]
        vn = [half(u_ref, hv) - ws[hv] for hv in heads]
        oi = [_dot(qk_ref[hk, c], jnp.concatenate([vn[2 * hk], vn[2 * hk + 1]], axis=0))
              for hk in range(GDN_HK)]
        kv = [_dot_tn(half(kd_ref, hv), vn[hv]) for hv in heads]
        for hv in heads:
            gl = gl_ref[hv // 2, c]
            s_ref[hv] = s[hv] * gl[hv % 2:hv % 2 + 1, :] + kv[hv]
            o_scr[c * C:(c + 1) * C, hv * GDN_DV:(hv + 1) * GDN_DV] = (
                qs[hv] + oi[hv // 2][(hv % 2) * C:(hv % 2 + 1) * C])
    _gated_norm_store(o_scr, z_ref, no_ref, og_ref)


def _gdn_scan(prep, proj, norm_o, layer, Bn, T):
    u, w, qd, kd, qk, gl = prep
    rows = GDN_SCAN_ROWS
    nblk = T // rows
    cpb = GDN_SCAN_CHUNKS
    bspec = pl.BlockSpec((None, GDN_HK, cpb, PAIR, LANES), lambda b, i: (b, 0, i, 0, 0))
    in_specs = [bspec, bspec, bspec, bspec, bspec,
                pl.BlockSpec((None, GDN_HK, cpb, 8, LANES), lambda b, i: (b, 0, i, 0, 0)),
                pl.BlockSpec((rows, GDN_VAL_DIM), lambda b, i: (b * nblk + i, GDN_QKV_DIM // GDN_VAL_DIM)),
                pl.BlockSpec((None, 1, GDN_DV), lambda b, i: (layer, 0, 0))]
    out_shape = (jax.ShapeDtypeStruct((Bn * T, GDN_VAL_DIM), BF16),
                 jax.ShapeDtypeStruct((Bn, GDN_HV, GDN_DK, GDN_DV), F32))
    out_specs = (pl.BlockSpec((rows, GDN_VAL_DIM), lambda b, i: (b * nblk + i, 0)),
                 pl.BlockSpec((None, GDN_HV, GDN_DK, GDN_DV), lambda b, i: (b, 0, 0, 0)))
    return pl.pallas_call(
        _gdn_scan_body,
        out_shape=out_shape,
        grid=(Bn, nblk),
        in_specs=in_specs,
        out_specs=out_specs,
        scratch_shapes=[pltpu.VMEM((rows, GDN_VAL_DIM), F32)],
        compiler_params=_cparams(("parallel", "arbitrary")),
        name="gdn_scan",
    )(u, w, qd, kd, qk, gl, proj, norm_o.reshape(-1, 1, GDN_DV))


GDN_STEP_ROWS = 8


def _gdn_step_body(proj_ref, ab_ref, cb_ref, cw_ref, alog_ref, dtb_ref, no_ref, st_ref,
                   og_ref, cbo_ref, sto_ref, o_scr):
    R = GDN_STEP_ROWS
    qkv = proj_ref[:, :GDN_QKV_DIM]
    cw = cw_ref[...]
    y = cb_ref[0] * cw[0:1, :]
    y = y + cb_ref[1] * cw[1:2, :]
    y = y + cb_ref[2] * cw[2:3, :]
    y = y + qkv * cw[3:4, :]
    qkv_c = _silu(y)
    cbo_ref[0] = cb_ref[1]
    cbo_ref[1] = cb_ref[2]
    cbo_ref[2] = qkv

    g16, be16 = _gdn_gates(ab_ref[...], alog_ref[...], dtb_ref[...])
    expand = ((_iota((GDN_HV, GDN_VAL_DIM), 1) >> 7) == _iota((GDN_HV, GDN_VAL_DIM), 0)).astype(BF16)
    eg_f = jnp.exp(_dot_exact_rhs(g16, expand))
    be_f = _dot_exact_rhs(be16, expand)
    r8 = _iota((8, LANES), 0)

    for hk in range(GDN_HK):
        qh = qkv_c[:, hk * GDN_DK:(hk + 1) * GDN_DK]
        kh = qkv_c[:, GDN_KEY_DIM + hk * GDN_DK:GDN_KEY_DIM + (hk + 1) * GDN_DK]
        qn = qh * lax.rsqrt(jnp.sum(qh * qh, axis=-1, keepdims=True) + EPS) * (GDN_DK ** -0.5)
        kn = kh * lax.rsqrt(jnp.sum(kh * kh, axis=-1, keepdims=True) + EPS)
        qk = jnp.sum(qn * kn, axis=-1, keepdims=True)
        for s in (0, 1):
            hv = 2 * hk + s
            sl = slice(hv * GDN_DV, (hv + 1) * GDN_DV)
            eg = eg_f[:, sl]
            be = be_f[:, sl]
            vh = qkv_c[:, 2 * GDN_KEY_DIM + hv * GDN_DV:2 * GDN_KEY_DIM + (hv + 1) * GDN_DV]
            w_rows = be * eg * kn
            qd_rows = qn * eg
            rows = range(R)
            st = [st_ref[bb, hv] for bb in rows]
            rr = [_dot(jnp.where(r8 == 0, w_rows[bb:bb + 1], jnp.where(r8 == 1, qd_rows[bb:bb + 1], 0.0)),
                       st[bb]) for bb in rows]
            vn = [be[bb:bb + 1] * vh[bb:bb + 1] - rr[bb][0:1] for bb in rows]
            kv = [_dot_tn(jnp.where(r8 == 0, kn[bb:bb + 1], 0.0), jnp.where(r8 == 0, vn[bb], 0.0))
                  for bb in rows]
            for bb in rows:
                sto_ref[bb, hv] = st[bb] * eg[bb:bb + 1] + kv[bb]
                o_scr[bb:bb + 1, sl] = rr[bb][1:2] + qk[bb:bb + 1] * vn[bb]
    _gated_norm_store(o_scr, proj_ref.at[:, GDN_QKV_DIM:], no_ref, og_ref)


def _gdn_step(proj, ab, conv_buf_t, conv_w, a_log, dt_bias, norm_o, state, layer, new_states=None):
    Bs = proj.shape[0]
    R = GDN_STEP_ROWS
    vec = lambda n: pl.BlockSpec((None, 1, n), lambda i: (layer, 0, 0))
    in_specs = [pl.BlockSpec((R, proj.shape[1]), lambda i: (i, 0)),
                pl.BlockSpec((R, 2 * GDN_HV), lambda i: (i, 0)),
                pl.BlockSpec((None, CONV_W - 1, R, GDN_QKV_DIM), lambda i: (layer, 0, i, 0)),
                pl.BlockSpec((None, CONV_W, GDN_QKV_DIM), lambda i: (layer, 0, 0)),
                vec(GDN_HV), vec(GDN_HV), vec(GDN_DV),
                pl.BlockSpec((None, R, GDN_HV, GDN_DK, GDN_DV), lambda i: (layer, i, 0, 0, 0))]
    out_shape = (jax.ShapeDtypeStruct((Bs, GDN_VAL_DIM), F32),
                 jax.ShapeDtypeStruct((CONV_W - 1, Bs, GDN_QKV_DIM), F32),
                 jax.ShapeDtypeStruct(state.shape, F32))
    out_specs = (pl.BlockSpec((R, GDN_VAL_DIM), lambda i: (i, 0)),
                 pl.BlockSpec((CONV_W - 1, R, GDN_QKV_DIM), lambda i: (0, i, 0)),
                 pl.BlockSpec((None, R, GDN_HV, GDN_DK, GDN_DV), lambda i: (layer, i, 0, 0, 0)))
    args = [proj, ab, conv_buf_t, conv_w, a_log.reshape(-1, 1, GDN_HV), dt_bias.reshape(-1, 1, GDN_HV),
            norm_o.reshape(-1, 1, GDN_DV), state]
    body, aliases = _gdn_step_body, {}
    if new_states is not None:
        in_specs.append(pl.BlockSpec(memory_space=pl.ANY))
        args.append(new_states)
        aliases = {len(args) - 1: 2}

        def body(*refs):
            n_in = len(args)
            _gdn_step_body(*refs[:n_in - 1], *refs[n_in:])

    return pl.pallas_call(
        body,
        out_shape=out_shape,
        grid=(Bs // R,),
        in_specs=in_specs,
        out_specs=out_specs,
        scratch_shapes=[pltpu.VMEM((R, GDN_VAL_DIM), F32)],
        input_output_aliases=aliases,
        compiler_params=_cparams(("parallel",)),
        name="gdn_step",
    )(*args)


def _rope_tables(ang):
    lane = _iota(ang.shape, 1)
    cs = jnp.cos(ang)
    sn = jnp.sin(ang)
    sg = jnp.where((lane & (ATT_DH - 1)) < ATT_DH // 2, -sn, sn)
    reps = ATT_DIM // LANES
    return jnp.concatenate([cs] * reps, axis=1), jnp.concatenate([sg] * reps, axis=1)


def _rope_apply(x, cs, sg):
    lane = _iota(x.shape, 1)
    first_half = (lane & (ATT_DH - 1)) < ATT_DH // 2
    half = ATT_DH // 2
    partner = jnp.where(first_half, pltpu.roll(x, x.shape[1] - half, 1), pltpu.roll(x, half, 1))
    return x * cs + partner * sg


def _rope_body(invf_ref, x_ref, o_ref, *, pos0, pos_step, row0, tr, col0):
    i = pl.program_id(1)
    c = pl.program_id(2)
    is_v = lax.rem(col0 + c, 3) == 2

    @pl.when(is_v)
    def _():
        o_ref[...] = x_ref[...].astype(o_ref.dtype)

    @pl.when(jnp.logical_not(is_v))
    def _():
        pos = (pos0 + pos_step * (row0 + i * tr + _iota((tr, LANES), 0))).astype(F32)
        cs, sg = _rope_tables(pos * invf_ref[...])
        o_ref[...] = _rope_apply(x_ref[...], cs, sg).astype(o_ref.dtype)


def _rope(proj, invf, Bn, T, row0, nrows, col0, ncols, out_dtype, pos0=0, pos_step=1):
    tr = min(nrows, 256)
    assert nrows % tr == 0 and row0 % tr == 0 and T % tr == 0
    nb = nrows // tr
    body = functools.partial(_rope_body, pos0=pos0, pos_step=pos_step, row0=row0, tr=tr, col0=col0)
    return pl.pallas_call(
        body,
        out_shape=jax.ShapeDtypeStruct((Bn * nrows, ncols * ATT_DIM), out_dtype),
        grid=(Bn, nb, ncols),
        in_specs=[pl.BlockSpec((1, LANES), lambda b, i, c: (0, 0)),
                  pl.BlockSpec((tr, ATT_DIM), lambda b, i, c: (b * (T // tr) + row0 // tr + i, col0 + c))],
        out_specs=pl.BlockSpec((tr, ATT_DIM), lambda b, i, c: (b * nb + i, c)),
        compiler_params=_cparams(("parallel", "parallel", "parallel")),
        name="rope",
    )(invf, proj)


def _attn_body(q_ref, kp_ref, kc_ref, vp_ref, vc_ref, o_ref, lse_ref):
    n = pl.program_id(2)
    blk = ATT_BLOCK
    qi = _iota((blk, 2 * blk), 0)
    kj = _iota((blk, 2 * blk), 1)
    dist = qi + blk - kj
    n_win = blk
    mask = (dist >= 0) & (dist <= n_win) & ((kj >= blk) | (n > 0))
    lane = _iota((blk, LANES), 1)
    lo = lane < ATT_DH
    lse_acc = jnp.zeros((blk, LANES), F32)
    scale = ATT_DH ** -0.5
    for hp in range(ATT_H // 2):
        sl = slice(hp * LANES, (hp + 1) * LANES)
        q2 = q_ref[:, sl].astype(F32)
        k2 = jnp.concatenate([kp_ref[:, sl], kc_ref[:, sl]], axis=0)
        v2 = jnp.concatenate([vp_ref[:, sl], vc_ref[:, sl]], axis=0)
        outs = []
        for s in (0, 1):
            qm = jnp.where(lo if s == 0 else jnp.logical_not(lo), q2, 0.0)
            sc = _dot_nt(qm, k2) * scale
            sc = jnp.where(mask, sc, NEG_INF)
            m = jnp.max(sc, axis=-1, keepdims=True)
            p = jnp.exp(sc - m)
            l = jnp.sum(p, axis=-1, keepdims=True)
            outs.append(_dot(p, v2) * (1.0 / l))
            lse_acc = jnp.where(lane == 2 * hp + s, m + jnp.log(l), lse_acc)
        o_ref[:, sl] = jnp.where(lo, outs[0], outs[1])
    lse_ref[...] = lse_acc


def _attn_prompt(qkv, gi, Bn, T):
    dil = DIL_GROUPS[gi][1]
    Td = T // dil
    nb = Td // ATT_BLOCK

    def spec(which, prev):
        def imap(b, r, n):
            return (b, r, jnp.maximum(n - 1, 0) if prev else n, which)
        return pl.BlockSpec((None, None, ATT_BLOCK, ATT_DIM), imap)

    return pl.pallas_call(
        _attn_body,
        out_shape=(jax.ShapeDtypeStruct((Bn, dil, Td, ATT_DIM), F32),
                   jax.ShapeDtypeStruct((Bn, dil, Td, LANES), F32)),
        grid=(Bn, dil, nb),
        in_specs=[spec(0, False), spec(1, True), spec(1, False), spec(2, True), spec(2, False)],
        out_specs=(pl.BlockSpec((None, None, ATT_BLOCK, ATT_DIM), lambda b, r, n: (b, r, n, 0)),
                   pl.BlockSpec((None, None, ATT_BLOCK, LANES), lambda b, r, n: (b, r, n, 0))),
        compiler_params=_cparams(("parallel", "parallel", "parallel")),
        name="attn_prompt",
    )(qkv, qkv, qkv, qkv, qkv)


ROPE_SPLIT_ROWS = 256


def _stride_perm(rows, dil, transpose=False):
    n = rows // dil
    i = _iota((rows, rows), 1 if transpose else 0)
    j = _iota((rows, rows), 0 if transpose else 1)
    shift = n.bit_length() - 1
    assert n == 1 << shift
    return (j == (i & (n - 1)) * dil + (i >> shift)).astype(BF16)


def _rope_split_body(invf_ref, x_ref, o0_ref, o1_ref, o2_ref):
    rows = ROPE_SPLIT_ROWS
    pos = (pl.program_id(1) * rows + _iota((rows, LANES), 0)).astype(F32)
    cs, sg = _rope_tables(pos * invf_ref[...])
    for gi, o_ref in enumerate((o0_ref, o1_ref, o2_ref)):
        dil = DIL_GROUPS[gi][1]
        n = rows // dil
        perm = _stride_perm(rows, dil) if dil > 1 else None
        for which in range(3):
            xx = x_ref[:, (3 * gi + which) * ATT_DIM:(3 * gi + which + 1) * ATT_DIM]
            if which < 2:
                xx = _rope_apply(xx, cs, sg)
            xb = xx.astype(BF16)
            if perm is not None:
                xb = jnp.dot(perm, xb, preferred_element_type=F32).astype(BF16)
            for r in range(dil):
                o_ref[r, :, which * ATT_DIM:(which + 1) * ATT_DIM] = xb[r * n:(r + 1) * n, :]


def _rope_split(proj, invf, Bn, T):
    rows = ROPE_SPLIT_ROWS
    nblk = T // rows
    out_shape, out_specs = [], []
    for _, dil in DIL_GROUPS:
        out_shape.append(jax.ShapeDtypeStruct((Bn, dil, T // dil, 3 * ATT_DIM), BF16))
        out_specs.append(pl.BlockSpec((None, dil, rows // dil, 3 * ATT_DIM), lambda b, i: (b, 0, i, 0)))
    return pl.pallas_call(
        _rope_split_body,
        out_shape=tuple(out_shape),
        grid=(Bn, nblk),
        in_specs=[pl.BlockSpec((1, LANES), lambda b, i: (0, 0)),
                  pl.BlockSpec((rows, 3 * N_GROUPS * ATT_DIM), lambda b, i: (b * nblk + i, 0))],
        out_specs=tuple(out_specs),
        compiler_params=_cparams(("parallel", "parallel")),
        name="rope_split",
    )(invf, proj)


MERGE_ROWS = 256


def _merge_body(o0_ref, o1_ref, o2_ref, l0_ref, l1_ref, l2_ref, out_ref):
    rows = MERGE_ROWS

    def position_order(ref, gi, passes):
        dil = DIL_GROUPS[gi][1]
        x = jnp.concatenate([ref[r] for r in range(dil)], axis=0)
        if dil == 1:
            return x
        return _dot_exact_lhs(_stride_perm(rows, dil, transpose=True), x, passes=passes)

    ls = [position_order(l_ref, gi, 3) for gi, l_ref in enumerate((l0_ref, l1_ref, l2_ref))]
    mx = jnp.maximum(jnp.maximum(ls[0], ls[1]), ls[2])
    es = [jnp.exp(l - mx) for l in ls]
    inv = 1.0 / (es[0] + es[1] + es[2])
    expand = ((_iota((LANES, ATT_DIM), 1) >> 6) == _iota((LANES, ATT_DIM), 0)).astype(BF16)
    acc = None
    for gi, o_ref in enumerate((o0_ref, o1_ref, o2_ref)):
        term = _dot_exact_rhs(es[gi] * inv, expand, passes=2) * position_order(o_ref, gi, 2)
        acc = term if acc is None else acc + term
    out_ref[...] = acc.astype(out_ref.dtype)


def _merge_groups(os, lses, Bn, T, out_dtype):
    rows = MERGE_ROWS
    nblk = T // rows
    ospecs, lspecs = [], []
    for _, dil in DIL_GROUPS:
        ospecs.append(pl.BlockSpec((None, dil, rows // dil, ATT_DIM), lambda b, i: (b, 0, i, 0)))
        lspecs.append(pl.BlockSpec((None, dil, rows // dil, LANES), lambda b, i: (b, 0, i, 0)))
    return pl.pallas_call(
        _merge_body,
        out_shape=jax.ShapeDtypeStruct((Bn * T, ATT_DIM), out_dtype),
        grid=(Bn, nblk),
        in_specs=ospecs + lspecs,
        out_specs=pl.BlockSpec((rows, ATT_DIM), lambda b, i: (b * nblk + i, 0)),
        compiler_params=_cparams(("parallel", "parallel")),
        name="attn_merge",
    )(*os, *lses)


ATT_STEP_CACHE_BLOCK_BYTES = 16 * 1024 * 1024


def _attn_step_body(q_ref, k_ref, v_ref, c_ref, o_ref, lse_ref, *, rows, dil):
    scale = ATT_DH ** -0.5
    L = c_ref.shape[-1]
    valid = (_iota((8, L), 1) & (dil - 1)) == 0
    r8 = _iota((8, ATT_DH), 0)
    r8l = _iota((8, L), 0)
    items = [(bb, hg) for bb in range(rows) for hg in range(ATT_H // 8)]
    hsl = lambda hg: slice(8 * hg, 8 * hg + 8)
    q8s = [q_ref[bb, hsl(hg), :] for bb, hg in items]
    s8s = []
    for (bb, hg), q8 in zip(items, q8s):
        s8 = jnp.zeros((8, L), F32)
        for hh in range(8):
            s8 = jnp.where(r8l == hh, _dot(q8, c_ref[bb, 0, 8 * hg + hh]), s8)
        s8s.append(jnp.where(valid, s8 * scale, NEG_INF))
    ps, pns, ls, ms = [], [], [], []
    for (bb, hg), q8, s8 in zip(items, q8s, s8s):
        sn = jnp.sum(q8 * k_ref[bb, hsl(hg), :], axis=-1, keepdims=True) * scale
        m = jnp.maximum(jnp.max(s8, axis=-1, keepdims=True), sn)
        p = jnp.exp(s8 - m)
        pn = jnp.exp(sn - m)
        ps.append(p)
        pns.append(pn)
        ms.append(m)
        ls.append(jnp.sum(p, axis=-1, keepdims=True) + pn)
    for (bb, hg), p, pn, l, m in zip(items, ps, pns, ls, ms):
        o8 = jnp.zeros((8, ATT_DH), F32)
        for hh in range(8):
            o8 = jnp.where(r8 == hh, _dot_nt(p, c_ref[bb, 1, 8 * hg + hh]), o8)
        o_ref[bb, hsl(hg), :] = (o8 + pn * v_ref[bb, hsl(hg), :]) * (1.0 / l)
        lse_ref[bb, hsl(hg), :] = jnp.broadcast_to(m + jnp.log(l), (8, ATT_DH))


def _attn_step(q3, k3, v3, cache, layer, gi):
    Bs = q3.shape[0]
    window, dil = DIL_GROUPS[gi]
    L = cache.shape[2]
    assert L == window and L % dil == 0 and dil & (dil - 1) == 0
    ct = jnp.transpose(cache, (0, 1, 3, 4, 5, 2))
    R = max(1, min(8, ATT_STEP_CACHE_BLOCK_BYTES // (2 * ATT_DIM * L * 4)))
    assert Bs % R == 0
    rspec = pl.BlockSpec((R, ATT_H, ATT_DH), lambda i: (i, 0, 0))
    return pl.pallas_call(
        functools.partial(_attn_step_body, rows=R, dil=dil),
        out_shape=(jax.ShapeDtypeStruct((Bs, ATT_H, ATT_DH), F32),
                   jax.ShapeDtypeStruct((Bs, ATT_H, ATT_DH), F32)),
        grid=(Bs // R,),
        in_specs=[rspec, rspec, rspec,
                  pl.BlockSpec((None, R, 2, ATT_H, ATT_DH, L), lambda i: (layer, i, 0, 0, 0, 0))],
        out_specs=(rspec, rspec),
        compiler_params=_cparams(("parallel",)),
        name="attn_step",
    )(q3, k3, v3, ct)


def _merge_step_body(o0_ref, o1_ref, o2_ref, l0_ref, l1_ref, l2_ref, out_ref):
    l0, l1, l2 = l0_ref[...], l1_ref[...], l2_ref[...]
    mx = jnp.maximum(jnp.maximum(l0, l1), l2)
    e0, e1, e2 = jnp.exp(l0 - mx), jnp.exp(l1 - mx), jnp.exp(l2 - mx)
    inv = 1.0 / (e0 + e1 + e2)
    out_ref[...] = (e0 * inv) * o0_ref[...] + (e1 * inv) * o1_ref[...] + (e2 * inv) * o2_ref[...]


def _merge_step(os, lses):
    shape = os[0].shape
    spec = pl.BlockSpec(shape, lambda i: (0, 0, 0))
    return pl.pallas_call(
        _merge_step_body,
        out_shape=jax.ShapeDtypeStruct(shape, F32),
        grid=(1,),
        in_specs=[spec] * 6,
        out_specs=spec,
        compiler_params=_cparams(("arbitrary",)),
        name="attn_merge_step",
    )(*os, *lses)


def _router_body(h_ref, w_ref, b_ref, o_ref):
    logits = _dot3(h_ref[...].astype(F32), w_ref[...]) + b_ref[...]
    lane = _iota(logits.shape, 1)
    logits = jnp.where(lane < N_EXPERTS, logits, -jnp.inf)
    m1 = jnp.max(logits, axis=-1, keepdims=True)
    i1 = jnp.min(jnp.where(logits == m1, lane, LANES), axis=-1, keepdims=True)
    rest = jnp.where(lane == i1, -jnp.inf, logits)
    m2 = jnp.max(rest, axis=-1, keepdims=True)
    i2 = jnp.min(jnp.where(rest == m2, lane, LANES), axis=-1, keepdims=True)
    e2 = jnp.exp(m2 - m1)
    g1 = 1.0 / (1.0 + e2)
    g2 = e2 * g1
    o_ref[...] = jnp.where(lane == i1, g1, 0.0) + jnp.where(lane == i2, g2, 0.0)


def _router(h, w_router, b_router):
    M, D = h.shape
    tm = min(M, 1024)
    wp = jnp.pad(w_router, ((0, 0), (0, LANES - N_EXPERTS)))
    bp = jnp.pad(b_router, (0, LANES - N_EXPERTS)).reshape(1, LANES)
    return pl.pallas_call(
        _router_body,
        out_shape=jax.ShapeDtypeStruct((M, LANES), F32),
        grid=(M // tm,),
        in_specs=[pl.BlockSpec((tm, D), lambda i: (i, 0)),
                  pl.BlockSpec((D, LANES), lambda i: (0, 0)),
                  pl.BlockSpec((1, LANES), lambda i: (0, 0))],
        out_specs=pl.BlockSpec((tm, LANES), lambda i: (i, 0)),
        compiler_params=_cparams(("parallel",)),
        name="router",
    )(h, wp, bp)


MOE_BLOCK = 512
MOE_TILE = 512
MOE_WINDOW = 128
MOE_NOT_ROUTED = -float(1 << 20)
MOE_LAST_FLAG = 4 << (MOE_TILE // MOE_WINDOW)


def _router_t_body(h_ref, wt_ref, b_ref, comb_ref, rank_ref, cnt_ref):
    wh, wl = _split2(wt_ref[...])
    hb = h_ref[...]
    logits = _dot_nt(wh, hb) + _dot_nt(wl, hb) + b_ref[...]
    row = _iota(logits.shape, 0)
    m1 = jnp.max(logits, axis=0, keepdims=True)
    i1 = jnp.min(jnp.where(logits == m1, row, N_EXPERTS), axis=0, keepdims=True)
    rest = jnp.where(row == i1, -jnp.inf, logits)
    m2 = jnp.max(rest, axis=0, keepdims=True)
    i2 = jnp.min(jnp.where(rest == m2, row, N_EXPERTS), axis=0, keepdims=True)
    e2 = jnp.exp(m2 - m1)
    g1 = 1.0 / (1.0 + e2)
    g2 = e2 * g1
    routed = (row == i1) | (row == i2)
    comb_ref[...] = jnp.where(row == i1, g1, 0.0) + jnp.where(row == i2, g2, 0.0)
    onehot = jnp.where(routed, 1.0, 0.0)
    tb = onehot.shape[1]
    before = (_iota((tb, tb), 0) < _iota((tb, tb), 1)).astype(BF16)
    rank = _dot(onehot, before)
    rank_ref[...] = jnp.where(routed, rank, MOE_NOT_ROUTED)
    cnt_ref[...] = jnp.broadcast_to(jnp.sum(onehot, axis=1, keepdims=True), cnt_ref.shape)


def _router_t(h, w_router, b_router):
    M, D = h.shape
    tb = MOE_BLOCK
    nblk = M // tb
    return pl.pallas_call(
        _router_t_body,
        out_shape=(jax.ShapeDtypeStruct((N_EXPERTS, M), F32), jax.ShapeDtypeStruct((N_EXPERTS, M), F32),
                   jax.ShapeDtypeStruct((nblk, N_EXPERTS, LANES), F32)),
        grid=(nblk,),
        in_specs=[pl.BlockSpec((tb, D), lambda i: (i, 0)),
                  pl.BlockSpec((N_EXPERTS, D), lambda i: (0, 0)),
                  pl.BlockSpec((N_EXPERTS, 1), lambda i: (0, 0))],
        out_specs=(pl.BlockSpec((N_EXPERTS, tb), lambda i: (0, i)),
                   pl.BlockSpec((N_EXPERTS, tb), lambda i: (0, i)),
                   pl.BlockSpec((None, N_EXPERTS, LANES), lambda i: (i, 0, 0))),
        compiler_params=_cparams(("parallel",)),
        name="router_t",
    )(h, jnp.transpose(w_router), b_router.reshape(N_EXPERTS, 1))


def _moe_work_lists(cnt, n_tiles):
    ts = MOE_TILE
    n_e, nblk = cnt.shape
    base = jnp.cumsum(cnt, axis=1) - cnt
    ntile = (jnp.sum(cnt, axis=1) + ts - 1) // ts
    tend = jnp.cumsum(ntile)
    tstart = tend - ntile
    q_lo = base // ts
    q_hi = (base + jnp.maximum(cnt, 1) - 1) // ts
    q = jnp.stack([q_lo, q_hi], axis=-1)
    valid = jnp.stack([cnt > 0, (cnt > 0) & (q_hi > q_lo)], axis=-1)
    tile = tstart[:, None, None] + q
    delta = q * ts - base[:, :, None]
    e_idx = jnp.broadcast_to(jnp.arange(n_e, dtype=jnp.int32)[:, None, None], q.shape)
    j_idx = jnp.broadcast_to(jnp.arange(nblk, dtype=jnp.int32)[None, :, None], q.shape)

    row0 = jnp.maximum(-delta, 0)
    row1 = jnp.minimum(cnt[:, :, None] - delta, ts)
    wbits = jnp.zeros_like(q)
    for w in range(ts // MOE_WINDOW):
        touched = (row0 < (w + 1) * MOE_WINDOW) & (row1 > w * MOE_WINDOW)
        wbits = wbits + touched.astype(q.dtype) * (4 << w)

    def ordered(perm, key):
        items = jnp.stack([jnp.transpose(a, perm).reshape(-1).astype(jnp.int32)
                           for a in (tile, j_idx, e_idx, delta, wbits, valid)])
        order = jnp.argsort(jnp.logical_not(items[5] > 0), stable=True)
        n_valid = jnp.sum(items[5])
        n_items = order.shape[0]
        pos = jnp.minimum(jnp.arange(n_items), jnp.maximum(n_valid - 1, 0))
        tile_s, j_s, e_s, d_s, w_s, _ = items[:, order[pos]]
        live = (jnp.arange(n_items) < n_valid).astype(jnp.int32)
        k = tile_s if key == "tile" else j_s
        first = jnp.concatenate([jnp.ones((1,), jnp.int32), (k[1:] != k[:-1]).astype(jnp.int32)])
        last = jnp.concatenate([first[1:], jnp.ones((1,), jnp.int32)])
        last = jnp.where(jnp.arange(n_items) == n_valid - 1, 1, last) * live
        return tile_s, j_s, e_s, d_s, first + live * (2 + w_s) + last * MOE_LAST_FLAG

    gather_list = ordered((0, 1, 2), "tile")
    combine_list = ordered((1, 0, 2), "blk")
    tile_expert = jnp.minimum(jnp.sum(jnp.arange(n_tiles)[:, None] >= tend[None, :], axis=1),
                              n_e - 1).astype(jnp.int32)
    return gather_list, combine_list, tile_expert


def _slot_selector(rank_ref, e, delta, w):
    tgt = rank_ref[pl.ds(e, 1), :] - (delta + w * MOE_WINDOW).astype(F32)
    r = _iota((MOE_WINDOW, tgt.shape[1]), 0).astype(F32)
    return jnp.where(r == tgt, 1.0, 0.0).astype(BF16)


def _moe_gather_body(tile_ref, blk_ref, e_ref, dlt_ref, flg_ref, h_ref, rank_ref, zero_ref, o_ref, acc_scr):
    s = pl.program_id(0)
    flags = flg_ref[s]

    @pl.when((flags & 1) == 1)
    def _():
        acc_scr[...] = jnp.zeros_like(acc_scr)

    for w in range(MOE_TILE // MOE_WINDOW):
        @pl.when((flags & (4 << w)) != 0)
        def _():
            sel = _slot_selector(rank_ref, e_ref[s], dlt_ref[s], w)
            rows = slice(w * MOE_WINDOW, (w + 1) * MOE_WINDOW)
            acc_scr[rows, :] += jnp.dot(sel, h_ref[...], preferred_element_type=F32)

    @pl.when((flags & MOE_LAST_FLAG) != 0)
    def _():
        o_ref[...] = acc_scr[...].astype(o_ref.dtype)


def _moe_gather(h, rank_t, work, n_tiles):
    M, D = h.shape
    tb, ts = MOE_BLOCK, MOE_TILE
    n_items = work[0].shape[0]
    zeros = jnp.zeros((n_tiles * ts, D), h.dtype)
    return pl.pallas_call(
        _moe_gather_body,
        out_shape=jax.ShapeDtypeStruct((n_tiles * ts, D), h.dtype),
        grid_spec=pltpu.PrefetchScalarGridSpec(
            num_scalar_prefetch=5, grid=(n_items,),
            in_specs=[pl.BlockSpec((tb, D), lambda s, t, b, e, d, f: (b[s], 0)),
                      pl.BlockSpec((N_EXPERTS, tb), lambda s, t, b, e, d, f: (0, b[s])),
                      pl.BlockSpec(memory_space=pl.ANY)],
            out_specs=pl.BlockSpec((ts, D), lambda s, t, b, e, d, f: (t[s], 0)),
            scratch_shapes=[pltpu.VMEM((ts, D), F32)]),
        input_output_aliases={7: 0},
        compiler_params=_cparams(("arbitrary",)),
        name="moe_gather",
    )(*work, h, rank_t, zeros)


def _moe_combine_body(tile_ref, blk_ref, e_ref, dlt_ref, flg_ref, ys_ref, rank_ref, comb_ref, x_ref, gate_ref,
                      o_ref, acc_scr):
    s = pl.program_id(0)
    flags = flg_ref[s]

    @pl.when((flags & 1) == 1)
    def _():
        acc_scr[...] = jnp.zeros_like(acc_scr)

    for w in range(MOE_TILE // MOE_WINDOW):
        @pl.when((flags & (4 << w)) != 0)
        def _():
            e = e_ref[s]
            sel = _slot_selector(rank_ref, e, dlt_ref[s], w)
            c3 = jnp.concatenate(_split3(comb_ref[pl.ds(e, 1), :]) + (jnp.zeros((5, sel.shape[1]), BF16),),
                                 axis=0)
            g = lax.dot_general(sel, c3, (((1,), (1,)), ((), ())), preferred_element_type=F32)
            gate_col = g[:, 0:1] + g[:, 1:2] + g[:, 2:3]
            ysg = ys_ref[w * MOE_WINDOW:(w + 1) * MOE_WINDOW, :] * gate_col
            acc_scr[...] += _dot_tn(sel, ysg)

    @pl.when((flags & MOE_LAST_FLAG) != 0)
    def _():
        o_ref[...] = x_ref[...] + gate_ref[...] * acc_scr[...]


def _moe_combine(ys, rank_t, comb_t, x, gate, work, rows_per_group):
    M, D = x.shape
    tb, ts = MOE_BLOCK, MOE_TILE
    n_items = work[0].shape[0]
    assert rows_per_group % tb == 0
    espec = pl.BlockSpec((N_EXPERTS, tb), lambda s, t, b, e, d, f: (0, b[s]))
    xspec = pl.BlockSpec((tb, D), lambda s, t, b, e, d, f: (b[s], 0))
    return pl.pallas_call(
        _moe_combine_body,
        out_shape=jax.ShapeDtypeStruct((M, D), F32),
        grid_spec=pltpu.PrefetchScalarGridSpec(
            num_scalar_prefetch=5, grid=(n_items,),
            in_specs=[pl.BlockSpec((ts, D), lambda s, t, b, e, d, f: (t[s], 0)),
                      espec, espec, xspec,
                      pl.BlockSpec((None, 1, D), lambda s, t, b, e, d, f: ((b[s] * tb) // rows_per_group, 0, 0))],
            out_specs=xspec,
            scratch_shapes=[pltpu.VMEM((tb, D), F32)]),
        compiler_params=_cparams(("arbitrary",)),
        name="moe_combine",
    )(*work, ys, rank_t, comb_t, x, gate.reshape(gate.shape[0], 1, D))


def _moe_top2(h, x, gate, w_router, b_router, w_gate, w_up, w_down, layer, rpg):
    M, D = x.shape
    n_tiles = 2 * M // MOE_TILE + N_EXPERTS
    comb_t, rank_t, cnt = _router_t(h, w_router, b_router)
    gather_list, combine_list, tile_expert = _moe_work_lists(
        jnp.transpose(cnt[:, :, 0]).astype(jnp.int32), n_tiles)
    xs = _moe_gather(h, rank_t, gather_list, n_tiles)
    act = _mm_grouped(xs, w_gate, layer, tile_expert, MOE_TILE, out_dtype=BF16, w2=w_up, name="moe_up")
    ys = _mm_grouped(act, w_down, layer, tile_expert, MOE_TILE, out_dtype=F32, name="moe_down")
    return _moe_combine(ys, rank_t, comb_t, x, gate, combine_list, rpg)


def _moe(h, x, gate, comb, w_gate, w_up, w_down, layer, rpg):
    d_ff = w_gate.shape[-1]
    D = x.shape[1]
    acc = None
    for e in range(N_EXPERTS):
        act = _mm(h, w_gate, (layer, e), d_ff, out_dtype=BF16, w2=w_up, name="moe_up")
        last = e == N_EXPERTS - 1
        acc = _mm(act, w_down, (layer, e), D, out_dtype=F32, rowscale=comb, rowscale_col=e, res=acc,
                  gate=gate if last else None, res2=x if last else None, rows_per_group=rpg,
                  name="moe_down")
    return acc


def _trunk(x, mod, rpg, Bn, T, is_prompt, state_gdn, state_conv_t, caches, invf, p):
    D = x.shape[1]
    d_ff = p["w_gate_d"].shape[-1]
    new_s, new_conv = [], []
    new_kv = [[] for _ in range(N_GROUPS)]
    for i in range(DEPTH):
        j = i // 2
        sh_m, sc_m, gt_m, sh_f, sc_f, gt_f = [mod[i][:, n * D:(n + 1) * D] for n in range(6)]
        h = _norm_mod(x, p["norm_mix"][i], sh_m, sc_m, rpg, BF16 if is_prompt else F32)
        if i % 2 == 0:
            proj = _mm(h, p["w_in_a"], (j,), GDN_QKV_DIM + GDN_VAL_DIM, out_dtype=F32, name="gdn_in")
            ab = _mm_small(h, p["w_in_a"][j][:, GDN_QKV_DIM + GDN_VAL_DIM:], "gdn_in_ab")
            if is_prompt:
                prep = _gdn_prep(proj, ab, p["conv_w_a"], p["a_log"], p["dt_bias"], j, Bn, T)
                og, S = _gdn_scan(prep, proj, p["norm_o_a"], j, Bn, T)
                tail = proj.reshape(Bn, T, -1)[:, T - (CONV_W - 1):, :GDN_QKV_DIM]
                new_conv.append(tail)
            else:
                og, cb_t, S = _gdn_step(proj, ab, state_conv_t, p["conv_w_a"], p["a_log"], p["dt_bias"],
                                        p["norm_o_a"], state_gdn, j, new_states=new_s[0] if new_s else None)
                new_conv.append(jnp.transpose(cb_t, (1, 0, 2)))
                new_s = []
            new_s.append(S)
            x = _mm(og, p["w_out_a"], (j,), D, out_dtype=F32, gate=gt_m, res2=x, rows_per_group=rpg,
                    name="gdn_out")
        else:
            proj = _mm(h, p["w_in_b"], (j,), 3 * N_GROUPS * ATT_DIM, out_dtype=F32, name="att_in")
            if is_prompt:
                qkvs = _rope_split(proj, invf, Bn, T)
                os, lses = [], []
                for gi, (window, dil) in enumerate(DIL_GROUPS):
                    o_g, lse_g = _attn_prompt(qkvs[gi], gi, Bn, T)
                    os.append(o_g)
                    lses.append(lse_g)
                    wn = min(window, T)
                    k_tail = _rope(proj, invf, Bn, T, T - wn, wn, 3 * gi + 1, 1, F32)
                    v_tail = proj.reshape(Bn, T, -1)[:, T - wn:, (3 * gi + 2) * ATT_DIM:(3 * gi + 3) * ATT_DIM]
                    new_kv[gi].append(jnp.stack(
                        [k_tail.reshape(Bn, wn, ATT_H, ATT_DH), v_tail.reshape(Bn, wn, ATT_H, ATT_DH)], axis=2))
                om = _merge_groups(os, lses, Bn, T, BF16)
            else:
                Bs = x.shape[0]
                pr = _rope(proj, invf, 1, Bs, 0, Bs, 0, 3 * N_GROUPS, F32, pos0=PAST_LEN, pos_step=0)
                os, lses = [], []
                for gi in range(N_GROUPS):
                    q3, k3, v3 = [pr[:, (3 * gi + n) * ATT_DIM:(3 * gi + n + 1) * ATT_DIM]
                                  .reshape(Bs, ATT_H, ATT_DH) for n in range(3)]
                    o_g, lse_g = _attn_step(q3, k3, v3, caches[gi], j, gi)
                    os.append(o_g)
                    lses.append(lse_g)
                    new_kv[gi].append(jnp.stack([k3, v3], axis=1).reshape(Bs, 1, 2, ATT_H, ATT_DH))
                om = _merge_step(os, lses).reshape(Bs, ATT_DIM)
            x = _mm(om, p["w_out_b"], (j,), D, out_dtype=F32, gate=gt_m, res2=x, rows_per_group=rpg,
                    name="att_out")
        h = _norm_mod(x, p["norm_ffn"][i], sh_f, sc_f, rpg, BF16 if is_prompt else F32)
        if i % 2 == 0:
            act = _mm(h, p["w_gate_d"], (j,), d_ff, out_dtype=BF16, w2=p["w_up_d"], name="ffn_up")
            x = _mm(act, p["w_down_d"], (j,), D, out_dtype=F32, gate=gt_f, res2=x, rows_per_group=rpg,
                    name="ffn_down")
        elif is_prompt:
            x = _moe_top2(h, x, gt_f, p["w_router"][j], p["b_router"][j],
                          p["w_gate_e"], p["w_up_e"], p["w_down_e"], j, rpg)
        else:
            comb = _router(h, p["w_router"][j], p["b_router"][j])
            x = _moe(h, x, gt_f, comb, p["w_gate_e"], p["w_up_e"], p["w_down_e"], j, rpg)
    y = _norm_mod(x, p["norm_final"], None, None, rpg, F32)
    return (y, jnp.stack(new_s) if is_prompt else new_s[0], jnp.stack(new_conv),
            jnp.stack(new_kv[0]), jnp.stack(new_kv[1]), jnp.stack(new_kv[2]))


def kernel(x_prompt, x_sample, state_gdn, state_conv, cache_kv_w128, cache_kv_w512, cache_kv_w2048,
           c_prompt, c_sample, w_ada, b_ada, norm_mix, norm_ffn, norm_final,
           w_in_a, conv_w_a, a_log, dt_bias, norm_o_a, w_out_a, w_in_b, w_out_b,
           w_gate_d, w_up_d, w_down_d, w_router, b_router, w_gate_e, w_up_e, w_down_e):
    Bp, T, D = x_prompt.shape
    Bs, Ts, _ = x_sample.shape
    assert Ts == 1 and cache_kv_w128.shape[2] == DIL_GROUPS[0][0]
    p = dict(norm_mix=norm_mix, norm_ffn=norm_ffn, norm_final=norm_final, w_in_a=w_in_a, conv_w_a=conv_w_a,
             a_log=a_log, dt_bias=dt_bias, norm_o_a=norm_o_a, w_out_a=w_out_a, w_in_b=w_in_b, w_out_b=w_out_b,
             w_gate_d=w_gate_d, w_up_d=w_up_d, w_down_d=w_down_d, w_router=w_router, b_router=b_router,
             w_gate_e=w_gate_e, w_up_e=w_up_e, w_down_e=w_down_e)

    n_c = Bp + Bs
    pad = (-n_c) % 8
    c_all = jnp.concatenate([c_prompt, c_sample, jnp.zeros((pad, D), F32)], axis=0)
    mods = [_mm(c_all, w_ada, (l,), 6 * D, out_dtype=F32, pre_silu=True, res=b_ada[l].reshape(1, 6 * D),
                rows_per_group=n_c + pad, name="adaln") for l in range(DEPTH)]
    mod_p = jnp.stack([m[:Bp] for m in mods])
    mod_s = jnp.stack([m[Bp:n_c] for m in mods])

    half = ATT_DH // 2
    inv_freq = ROPE_THETA ** (-jnp.arange(half, dtype=F32) / half)
    invf = jnp.tile(inv_freq, LANES // half).reshape(1, LANES)

    y_p, s_p, cv_p, kv0_p, kv1_p, kv2_p = _trunk(
        x_prompt.reshape(Bp * T, D), mod_p, T, Bp, T, True, None, None, None, invf, p)
    y_s, s_s, cv_s, kv0_s, kv1_s, kv2_s = _trunk(
        x_sample.reshape(Bs, D), mod_s, 1, Bs, 1, False, state_gdn,
        jnp.transpose(state_conv, (0, 2, 1, 3)), (cache_kv_w128, cache_kv_w512, cache_kv_w2048), invf, p)
    return (y_p.reshape(Bp, T, D), y_s.reshape(Bs, Ts, D), s_p, cv_p, kv0_p, kv1_p, kv2_p,
            s_s, cv_s, kv0_s, kv1_s, kv2_s)
```
